```python
import math
import jax, jax.numpy as jnp
from jax import lax
import numpy as np

D_MODEL = 4096
BATCH = 8
SEQ = 2048
DEPTH = 4
DEC_BATCH = 2
DEC_SEQ = 8192
PAST_LEN = 128

MIX_WIDTH = D_MODEL
N_DIR = 2
S5_WIDTH = MIX_WIDTH // 4
S5_GROUP_CH = 16
S5_GROUPS = S5_WIDTH // S5_GROUP_CH
S5_STATE = 64
GLA_WIDTH = 3 * MIX_WIDTH // 8
GLA_DV = 128
GLA_DK = GLA_DV // 2
GLA_HEADS = GLA_WIDTH // GLA_DV
GLA_LOW_RANK = 16
GLA_GATE_NORM = 16.0
GDN_WIDTH = MIX_WIDTH - S5_WIDTH - GLA_WIDTH
GDN_DK = 128
GDN_DV = 128
GDN_HEADS = GDN_WIDTH // GDN_DV
GDN_QKV = GDN_HEADS * (2 * GDN_DK + GDN_DV)
GDN_CONV = 5
CHUNK = 64
D_FF = 256 * ((8 * D_MODEL // 3 + 255) // 256)
FFN_CONV = 3
NORM_EPS = 1e-6
IN_WIDTHS = (S5_WIDTH, GLA_HEADS * GLA_DK, GLA_HEADS * GLA_DK, GLA_WIDTH, GLA_WIDTH, N_DIR * GLA_LOW_RANK,
             GDN_QKV, GDN_WIDTH, N_DIR * GDN_HEADS, N_DIR * GDN_HEADS)
N_IN = sum(IN_WIDTHS)
IN_SPLITS = tuple(int(s) for s in np.cumsum(IN_WIDTHS)[:-1])

kernel_name = "hybrid_bidir_s5_gla_gdn_encoder"

F32 = jnp.float32


def _rms_norm(x, gain):
    xf = x.astype(F32)
    y = xf * lax.rsqrt(jnp.mean(xf * xf, axis=-1, keepdims=True) + NORM_EPS)
    return (y * gain.astype(F32)).astype(x.dtype)


def _l2_norm(x):
    return x * lax.rsqrt(jnp.sum(x * x, axis=-1, keepdims=True) + NORM_EPS)


def _depthwise_conv(x, w):
    pad = w.shape[0] // 2
    return lax.conv_general_dilated(x, w.astype(x.dtype), (1,), ((pad, pad),),
                                    dimension_numbers=('NWC', 'WIO', 'NWC'),
                                    feature_group_count=x.shape[-1])


def _chunk_mask(strict):
    idx = jnp.arange(CHUNK)
    return (idx[:, None] > idx[None, :]) if strict else (idx[:, None] >= idx[None, :])


def _to_chunks(t):
    b, l, h, d = t.shape
    return t.reshape(b, l // CHUNK, CHUNK, h, d).transpose(1, 0, 3, 2, 4)


def _scalar_to_chunks(t):
    b, l, h = t.shape
    return t.reshape(b, l // CHUNK, CHUNK, h).transpose(1, 0, 3, 2)


def _from_chunks(t):
    n, b, h, c, d = t.shape
    return t.transpose(1, 0, 3, 2, 4).reshape(b, n * c, h, d)


def _rev(t):
    return jnp.flip(t, axis=1)


def _complex_affine_combine(e1, e2):
    a1r, a1i, b1r, b1i = e1
    a2r, a2i, b2r, b2i = e2
    return (a2r * a1r - a2i * a1i,
            a2r * a1i + a2i * a1r,
            a2r * b1r - a2i * b1i + b2r,
            a2r * b1i + a2i * b1r + b2i)


def _s5_mixer(u, lam_re, lam_im, log_step, b_re, b_im, c_re, c_im, d_skip, glu_w, glu_b):
    bsz, l, _ = u.shape
    uf = u.astype(F32).reshape(bsz, l, S5_GROUPS, S5_GROUP_CH)
    y = uf * d_skip.astype(F32).reshape(S5_GROUPS, S5_GROUP_CH)
    for d, reverse in ((0, False), (1, True)):
        lr = lam_re[d].astype(F32)
        li = lam_im[d].astype(F32)
        delta = jnp.exp(log_step[d].astype(F32))[:, None]
        mag = jnp.exp(lr * delta)
        abar_re = mag * jnp.cos(li * delta)
        abar_im = mag * jnp.sin(li * delta)
        den = lr * lr + li * li
        nr = abar_re - 1.0
        ni = abar_im
        fr = (nr * lr + ni * li) / den
        fi = (ni * lr - nr * li) / den
        br = b_re[d].astype(F32)
        bi = b_im[d].astype(F32)
        bb_re = fr[..., None] * br - fi[..., None] * bi
        bb_im = fr[..., None] * bi + fi[..., None] * br
        bu_re = jnp.einsum('blgh,gph->blgp', uf, bb_re)
        bu_im = jnp.einsum('blgh,gph->blgp', uf, bb_im)
        a_re = jnp.broadcast_to(abar_re, bu_re.shape)
        a_im = jnp.broadcast_to(abar_im, bu_im.shape)
        _, _, s_re, s_im = lax.associative_scan(_complex_affine_combine, (a_re, a_im, bu_re, bu_im),
                                                axis=1, reverse=reverse)
        y = y + jnp.einsum('blgp,ghp->blgh', s_re, c_re[d].astype(F32)) \
              - jnp.einsum('blgp,ghp->blgh', s_im, c_im[d].astype(F32))
    y = jax.nn.gelu(y.reshape(bsz, l, S5_WIDTH))
    out = y * jax.nn.sigmoid(y @ glu_w.astype(F32) + glu_b.astype(F32))
    return out.astype(u.dtype)


def _gla_chunked(q, k, v, log_a, strict):
    dtype = v.dtype
    q, k, v, log_a = (_to_chunks(t.astype(F32)) for t in (q, k, v, log_a))
    b = jnp.cumsum(log_a, axis=-2)
    q_dec = q * jnp.exp(b)
    attn = jnp.where(_chunk_mask(strict), jnp.einsum('nbhcd,nbhsd->nbhcs', q_dec, k * jnp.exp(-b)), 0.0)
    o_intra = jnp.einsum('nbhcs,nbhse->nbhce', attn, v)
    b_last = b[..., -1:, :]
    kv = jnp.einsum('nbhcd,nbhce->nbhde', k * jnp.exp(b_last - b), v)
    chunk_decay = jnp.exp(b_last[..., 0, :])

    def step(state, xs):
        q_n, kv_n, dec_n = xs
        o_n = jnp.einsum('bhcd,bhde->bhce', q_n, state)
        state = state * dec_n[..., None] + kv_n
        return state, o_n

    state0 = jnp.zeros(q.shape[1:3] + (q.shape[-1], v.shape[-1]), F32)
    _, o_inter = lax.scan(step, state0, (q_dec, kv, chunk_decay))
    return _from_chunks(o_intra + o_inter).astype(dtype)


def _gla_mixer(q, k, v, g, lr, gate_up, gate_bias, norm_gain):
    bsz, l, _ = q.shape
    q = q.reshape(bsz, l, GLA_HEADS, GLA_DK) * (GLA_DK ** -0.5)
    k = k.reshape(bsz, l, GLA_HEADS, GLA_DK)
    v = v.reshape(bsz, l, GLA_HEADS, GLA_DV)
    lr = lr.reshape(bsz, l, N_DIR, GLA_LOW_RANK)

    def log_decay(d):
        z = jnp.einsum('blr,rk->blk', lr[:, :, d], gate_up[d]) + gate_bias[d]
        return (jax.nn.log_sigmoid(z.astype(F32)) / GLA_GATE_NORM).reshape(bsz, l, GLA_HEADS, GLA_DK)

    o_fwd = _gla_chunked(q, k, v, log_decay(0), strict=False)
    o_bwd = _rev(_gla_chunked(_rev(q), _rev(k), _rev(v), _rev(log_decay(1)), strict=True))
    o = _rms_norm(o_fwd + o_bwd, norm_gain) * jax.nn.silu(g.reshape(bsz, l, GLA_HEADS, GLA_DV))
    return o.reshape(bsz, l, GLA_WIDTH).astype(q.dtype)


def _gdn_chunked(q, k, v, beta, g, strict):
    dtype = v.dtype
    q, k, v = (_to_chunks(t.astype(F32)) for t in (q, k, v))
    beta, g = (_scalar_to_chunks(t.astype(F32)) for t in (beta, g))
    g_cum = jnp.cumsum(g, axis=-1)
    decay = jnp.exp(jnp.where(_chunk_mask(False), g_cum[..., :, None] - g_cum[..., None, :], -jnp.inf))
    k_beta = k * beta[..., None]
    a_mat = jnp.where(_chunk_mask(True), jnp.einsum('nbhcd,nbhsd->nbhcs', k_beta, k) * decay, 0.0)
    eye_plus_a = a_mat + jnp.eye(CHUNK, dtype=F32)
    u = lax.linalg.triangular_solve(eye_plus_a, v * beta[..., None], left_side=True, lower=True)
    w = lax.linalg.triangular_solve(eye_plus_a, k_beta * jnp.exp(g_cum)[..., None], left_side=True, lower=True)
    qk = jnp.where(_chunk_mask(strict), jnp.einsum('nbhcd,nbhsd->nbhcs', q, k) * decay, 0.0)
    q_dec = q * jnp.exp(g_cum)[..., None]
    k_dec = k * jnp.exp(g_cum[..., -1:] - g_cum)[..., None]
    chunk_decay = jnp.exp(g_cum[..., -1])

    def step(state, xs):
        q_n, k_n, u_n, w_n, qk_n, dec_n = xs
        v_new = u_n - jnp.einsum('bhcd,bhde->bhce', w_n, state)
        o_n = jnp.einsum('bhcd,bhde->bhce', q_n, state) + jnp.einsum('bhcs,bhse->bhce', qk_n, v_new)
        state = state * dec_n[..., None, None] + jnp.einsum('bhcd,bhce->bhde', k_n, v_new)
        return state, o_n

    state0 = jnp.zeros(q.shape[1:3] + (GDN_DK, GDN_DV), F32)
    _, o = lax.scan(step, state0, (q_dec, k_dec, u, w, qk, chunk_decay))
    return _from_chunks(o).astype(dtype)


def _gdn_mixer(qkv, z, beta_logit, a_logit, conv_w, a_log, dt_bias, norm_gain):
    bsz, l, _ = qkv.shape
    qkv = jax.nn.silu(_depthwise_conv(qkv, conv_w))
    q, k, v = jnp.split(qkv, [GDN_HEADS * GDN_DK, 2 * GDN_HEADS * GDN_DK], axis=-1)
    q = _l2_norm(q.reshape(bsz, l, GDN_HEADS, GDN_DK).astype(F32)) * (GDN_DK ** -0.5)
    k = _l2_norm(k.reshape(bsz, l, GDN_HEADS, GDN_DK).astype(F32))
    v = v.reshape(bsz, l, GDN_HEADS, GDN_DV)
    beta_logit = beta_logit.reshape(bsz, l, N_DIR, GDN_HEADS).astype(F32)
    a_logit = a_logit.reshape(bsz, l, N_DIR, GDN_HEADS).astype(F32)

    def gates(d):
        beta = jax.nn.sigmoid(beta_logit[:, :, d])
        g = -jnp.exp(a_log[d].astype(F32)) * jax.nn.softplus(a_logit[:, :, d] + dt_bias[d].astype(F32))
        return beta, g

    beta_f, g_f = gates(0)
    beta_b, g_b = gates(1)
    o_fwd = _gdn_chunked(q, k, v, beta_f, g_f, strict=False)
    o_bwd = _rev(_gdn_chunked(_rev(q), _rev(k), _rev(v), _rev(beta_b), _rev(g_b), strict=True))
    o = _rms_norm(o_fwd + o_bwd, norm_gain) * jax.nn.silu(z.reshape(bsz, l, GDN_HEADS, GDN_DV))
    return o.reshape(bsz, l, GDN_WIDTH).astype(qkv.dtype)


def _layer(x, c, ada_w, ada_b, norm_gains, w_in,
           s5_lambda_re, s5_lambda_im, s5_log_step, s5_b_re, s5_b_im, s5_c_re, s5_c_im, s5_d, s5_glu_w, s5_glu_b,
           gla_gate_up, gla_gate_bias, gla_norm,
           gdn_conv, gdn_a_log, gdn_dt_bias, gdn_norm,
           w_out, ffn_up, ffn_conv, ffn_down):
    mod = jax.nn.silu(c) @ ada_w + ada_b
    shift1, scale1, gate1, shift2, scale2, gate2 = [m[:, None, :] for m in jnp.split(mod, 6, axis=-1)]
    h = _rms_norm(x, norm_gains[0]) * (1 + scale1) + shift1
    (s5_u, gla_q, gla_k, gla_v, gla_g, gla_lr,
     gdn_qkv, gdn_z, gdn_beta, gdn_a) = jnp.split(h @ w_in, IN_SPLITS, axis=-1)
    y_s5 = _s5_mixer(s5_u, s5_lambda_re, s5_lambda_im, s5_log_step, s5_b_re, s5_b_im,
                     s5_c_re, s5_c_im, s5_d, s5_glu_w, s5_glu_b)
    y_gla = _gla_mixer(gla_q, gla_k, gla_v, gla_g, gla_lr, gla_gate_up, gla_gate_bias, gla_norm)
    y_gdn = _gdn_mixer(gdn_qkv, gdn_z, gdn_beta, gdn_a, gdn_conv, gdn_a_log, gdn_dt_bias, gdn_norm)
    mixed = jnp.concatenate([y_s5.astype(x.dtype), y_gla.astype(x.dtype), y_gdn.astype(x.dtype)], axis=-1) @ w_out
    x = x + gate1 * _rms_norm(mixed, norm_gains[1])
    h = _rms_norm(x, norm_gains[2]) * (1 + scale2) + shift2
    gate, val = jnp.split(_depthwise_conv(h @ ffn_up, ffn_conv), 2, axis=-1)
    f = (jax.nn.silu(gate) * val) @ ffn_down
    x = x + gate2 * _rms_norm(f, norm_gains[3])
    return x


def _trunk(x, c, weights):
    for layer_idx in range(DEPTH):
        x = _layer(x, c, *[w[layer_idx] for w in weights])
    return x


def setup_inputs(seed: int = 0) -> dict:
    key = jax.random.key(seed)
    ks = iter(jax.random.split(key, 29))

    def nrm(shape, scale):
        return jax.random.normal(next(ks), shape, F32) * scale

    def unif(shape, lo, hi):
        return jax.random.uniform(next(ks), shape, F32, lo, hi)

    lam_im_base = jnp.pi * jnp.arange(S5_STATE, dtype=F32)
    inp = {}
    inp['x_prompt'] = nrm((BATCH, SEQ, D_MODEL), 1.0)
    inp['x_sample'] = nrm((DEC_BATCH, DEC_SEQ, D_MODEL), 1.0)
    inp['c_prompt'] = nrm((BATCH, D_MODEL), 1.0)
    inp['c_sample'] = nrm((DEC_BATCH, D_MODEL), 1.0)
    inp['ada_w'] = nrm((DEPTH, D_MODEL, 6 * D_MODEL), D_MODEL ** -0.5)
    inp['ada_b'] = nrm((DEPTH, 6 * D_MODEL), 0.02)
    inp['norm_gains'] = 1.0 + nrm((DEPTH, 4, D_MODEL), 0.05)
    inp['w_in'] = nrm((DEPTH, D_MODEL, N_IN), D_MODEL ** -0.5)
    inp['s5_lambda_re'] = -0.5 + nrm((DEPTH, N_DIR, S5_GROUPS, S5_STATE), 0.01)
    inp['s5_lambda_im'] = lam_im_base + nrm((DEPTH, N_DIR, S5_GROUPS, S5_STATE), 0.01)
    inp['s5_log_step'] = unif((DEPTH, N_DIR, S5_GROUPS), math.log(1e-3), math.log(1e-1))
    inp['s5_b_re'] = nrm((DEPTH, N_DIR, S5_GROUPS, S5_STATE, S5_GROUP_CH), (2 * S5_GROUP_CH) ** -0.5)
    inp['s5_b_im'] = nrm((DEPTH, N_DIR, S5_GROUPS, S5_STATE, S5_GROUP_CH), (2 * S5_GROUP_CH) ** -0.5)
    inp['s5_c_re'] = nrm((DEPTH, N_DIR, S5_GROUPS, S5_GROUP_CH, S5_STATE), S5_STATE ** -0.5)
    inp['s5_c_im'] = nrm((DEPTH, N_DIR, S5_GROUPS, S5_GROUP_CH, S5_STATE), S5_STATE ** -0.5)
    inp['s5_d'] = nrm((DEPTH, S5_WIDTH), 1.0)
    inp['s5_glu_w'] = nrm((DEPTH, S5_WIDTH, S5_WIDTH), S5_WIDTH ** -0.5)
    inp['s5_glu_b'] = nrm((DEPTH, S5_WIDTH), 0.02)
    inp['gla_gate_up'] = nrm((DEPTH, N_DIR, GLA_LOW_RANK, GLA_HEADS * GLA_DK), GLA_LOW_RANK ** -0.5)
    inp['gla_gate_bias'] = nrm((DEPTH, N_DIR, GLA_HEADS * GLA_DK), 0.1)
    inp['gla_norm'] = 1.0 + nrm((DEPTH, GLA_DV), 0.05)
    inp['gdn_conv'] = nrm((DEPTH, GDN_CONV, 1, GDN_QKV), GDN_CONV ** -0.5)
    inp['gdn_a_log'] = jnp.log(unif((DEPTH, N_DIR, GDN_HEADS), 1.0, 16.0))
    dt = jnp.exp(unif((DEPTH, N_DIR, GDN_HEADS), math.log(1e-3), math.log(1e-1)))
    inp['gdn_dt_bias'] = dt + jnp.log(-jnp.expm1(-dt))
    inp['gdn_norm'] = 1.0 + nrm((DEPTH, GDN_DV), 0.05)
    inp['w_out'] = nrm((DEPTH, MIX_WIDTH, D_MODEL), MIX_WIDTH ** -0.5)
    inp['ffn_up'] = nrm((DEPTH, D_MODEL, 2 * D_FF), D_MODEL ** -0.5)
    inp['ffn_conv'] = nrm((DEPTH, FFN_CONV, 1, 2 * D_FF), FFN_CONV ** -0.5)
    inp['ffn_down'] = nrm((DEPTH, D_FF, D_MODEL), D_FF ** -0.5)
    return inp


def reference(x_prompt, x_sample, c_prompt, c_sample, ada_w, ada_b, norm_gains, w_in,
              s5_lambda_re, s5_lambda_im, s5_log_step, s5_b_re, s5_b_im, s5_c_re, s5_c_im, s5_d, s5_glu_w, s5_glu_b,
              gla_gate_up, gla_gate_bias, gla_norm,
              gdn_conv, gdn_a_log, gdn_dt_bias, gdn_norm,
              w_out, ffn_up, ffn_conv, ffn_down):
    weights = (ada_w, ada_b, norm_gains, w_in,
               s5_lambda_re, s5_lambda_im, s5_log_step, s5_b_re, s5_b_im, s5_c_re, s5_c_im, s5_d, s5_glu_w, s5_glu_b,
               gla_gate_up, gla_gate_bias, gla_norm,
               gdn_conv, gdn_a_log, gdn_dt_bias, gdn_norm,
               w_out, ffn_up, ffn_conv, ffn_down)
    y_prompt = _trunk(x_prompt, c_prompt, weights)
    y_sample = _trunk(x_sample, c_sample, weights)
    return (y_prompt, y_sample)
```

```python
import functools
import math
from typing import NamedTuple

import jax
import jax.numpy as jnp
import numpy as np
from jax import lax
from jax.experimental import pallas as pl
from jax.experimental.pallas import tpu as pltpu

F32 = jnp.float32
BF16 = jnp.bfloat16
HIGHEST = lax.Precision.HIGHEST

NORM_EPS = 1e-6
CHUNK = 64
LANES = 128
SUBLANES = 8
N_DIR = 2
S5_GROUP_CH = 16
S5_STATE = 64
S5_BLOCK_GROUPS = LANES // S5_GROUP_CH
S5_BLOCK_STATES = S5_BLOCK_GROUPS * S5_STATE
GLA_DV = 128
GLA_DK = 64
GLA_LOW_RANK = 16
GLA_GATE_NORM = 16.0
GDN_DK = 128
GDN_DV = 128
GDN_CONV = 5
FFN_CONV = 3
VMEM_LIMIT_BYTES = 56 * 1024 * 1024
MM_TILE = 1024


class _Seq(NamedTuple):
    b1: int
    l1: int
    b2: int
    l2: int

    @property
    def tokens(self):
        return self.b1 * self.l1 + self.b2 * self.l2

    @property
    def nseq(self):
        return self.b1 + self.b2


class _Dims(NamedTuple):
    d: int
    s5w: int
    gla_w: int
    gla_h: int
    gdn_w: int
    gdn_h: int
    dff: int
    dffp: int
    o_qkv: int
    o_z: int
    o_gv: int
    o_gg: int
    o_gq: int
    o_gk: int
    o_small: int
    o_s5: int
    n_in_pad: int


def _round_up(x, m):
    return (x + m - 1) // m * m


def _make_dims(d):
    s5w = d // 4
    gla_w = 3 * d // 8
    gla_h = gla_w // GLA_DV
    gdn_w = d - s5w - gla_w
    gdn_h = gdn_w // GDN_DV
    dff = 256 * ((8 * d // 3 + 255) // 256)
    dffp = _round_up(dff, MM_TILE)
    segs = [3 * gdn_w, gdn_w, gla_w, gla_w, gla_h * GLA_DK, gla_h * GLA_DK, LANES, s5w]
    offs = []
    cur = 0
    for w in segs:
        cur = _round_up(cur, w)
        offs.append(cur)
        cur += w
    return _Dims(d, s5w, gla_w, gla_h, gdn_w, gdn_h, dff, dffp, *offs, _round_up(cur, MM_TILE))


def _tile_seq(i, tm, sq):
    p1 = sq.l1 // tm
    p2 = sq.l2 // tm
    n1 = sq.b1 * p1
    in1 = i < n1
    j = jnp.maximum(i - n1, 0)
    seq = jnp.where(in1, i // p1, sq.b1 + j // p2)
    pos = jnp.where(in1, i % p1, j % p2)
    last = jnp.where(in1, p1 - 1, p2 - 1)
    return seq, pos == 0, pos == last


def _cparams(*sem):
    return pltpu.CompilerParams(dimension_semantics=sem, vmem_limit_bytes=VMEM_LIMIT_BYTES)


def _dot(a, b, **kw):
    return jnp.dot(a, b, preferred_element_type=F32, **kw)


def _dot_nt(a, b, **kw):
    return lax.dot_general(a, b, (((1,), (1,)), ((), ())), preferred_element_type=F32, **kw)


def _dot_tn(a, b, **kw):
    return lax.dot_general(a, b, (((0,), (0,)), ((), ())), preferred_element_type=F32, **kw)


def _softplus(x):
    return jnp.maximum(x, 0.0) + jnp.log1p(jnp.exp(-jnp.abs(x)))


def _log_sigmoid(x):
    return jnp.minimum(x, 0.0) - jnp.log1p(jnp.exp(-jnp.abs(x)))


def _silu(x):
    return x * jax.nn.sigmoid(x)


def _mod_kernel(c_ref, w_ref, b_ref, o_ref):
    a = _silu(c_ref[...]).astype(BF16)
    o_ref[0] = _dot(a, w_ref[0].astype(BF16)) + b_ref[0]


def _mod_call(c_rows, ada_w, ada_b):
    nl, d, n = ada_w.shape
    r = c_rows.shape[0]
    tn = min(512, n)
    return pl.pallas_call(
        _mod_kernel,
        grid=(nl, n // tn),
        in_specs=[pl.BlockSpec((r, d), lambda l, j: (0, 0)),
                  pl.BlockSpec((1, d, tn), lambda l, j: (l, 0, j)),
                  pl.BlockSpec((1, 1, tn), lambda l, j: (l, 0, j))],
        out_specs=pl.BlockSpec((1, r, tn), lambda l, j: (l, 0, j)),
        out_shape=jax.ShapeDtypeStruct((nl, r, n), F32),
        compiler_params=_cparams("parallel", "parallel"),
        name="adaln_mod",
    )(c_rows, ada_w, ada_b.reshape(nl, 1, n))


def _resnorm_kernel(*refs, has_res, has_norm):
    it = iter(refs)
    x_ref = next(it)
    if has_res:
        m_ref, gate_ref, ga_ref = next(it), next(it), next(it)
    if has_norm:
        gb_ref, scale_ref, shift_ref = next(it), next(it), next(it)
    if has_res:
        xo_ref = next(it)
    if has_norm:
        h_ref = next(it)
    x = x_ref[...]
    if has_res:
        m = m_ref[...]
        r = lax.rsqrt(jnp.mean(m * m, axis=-1, keepdims=True) + NORM_EPS)
        x = x + gate_ref[0] * (m * r * ga_ref[0])
        xo_ref[...] = x
    if has_norm:
        r = lax.rsqrt(jnp.mean(x * x, axis=-1, keepdims=True) + NORM_EPS)
        h = (x * r * gb_ref[0]) * (1.0 + scale_ref[0]) + shift_ref[0]
        h_ref[...] = h.astype(BF16)


def _resnorm_call(x, sq, *, res=None, norm=None):
    t, d = x.shape
    tm = min(256, sq.l1, sq.l2)
    row = pl.BlockSpec((tm, d), lambda i: (i, 0))

    def mod_spec(col):
        return pl.BlockSpec((1, 1, d), lambda i: (_tile_seq(i, tm, sq)[0], 0, col))

    def gain_spec(r):
        return pl.BlockSpec((1, 1, d), lambda i: (r, 0, 0))

    args, in_specs, out_shape, out_specs = [x], [row], [], []
    if res is not None:
        m, mod3, gate_col, gains3, gain_row = res
        args += [m, mod3, gains3]
        in_specs += [row, mod_spec(gate_col), gain_spec(gain_row)]
        out_shape.append(jax.ShapeDtypeStruct((t, d), F32))
        out_specs.append(row)
    if norm is not None:
        mod3, scale_col, shift_col, gains3, gain_row = norm
        args += [gains3, mod3, mod3]
        in_specs += [gain_spec(gain_row), mod_spec(scale_col), mod_spec(shift_col)]
        out_shape.append(jax.ShapeDtypeStruct((t, d), BF16))
        out_specs.append(row)
    return pl.pallas_call(
        functools.partial(_resnorm_kernel, has_res=res is not None, has_norm=norm is not None),
        grid=(t // tm,),
        in_specs=in_specs, out_specs=out_specs, out_shape=out_shape,
        compiler_params=_cparams("parallel"),
        name="resnorm",
    )(*args)


def _mm_kernel(x_ref, w_ref, o_ref):
    o_ref[...] = _dot(x_ref[...], w_ref[...]).astype(o_ref.dtype)


def _mm_call(x, w, out_dtype, name):
    m, k = x.shape
    n = w.shape[1]
    tm = min(MM_TILE, m)
    tn = min(MM_TILE, n)
    return pl.pallas_call(
        _mm_kernel,
        grid=(m // tm, n // tn),
        in_specs=[pl.BlockSpec((tm, k), lambda i, j: (i, 0)),
                  pl.BlockSpec((k, tn), lambda i, j: (0, j))],
        out_specs=pl.BlockSpec((tm, tn), lambda i, j: (i, j)),
        out_shape=jax.ShapeDtypeStruct((m, n), out_dtype),
        compiler_params=_cparams("parallel", "arbitrary"),
        name=name,
    )(x, w)


def _mm_kacc_kernel(x_ref, w_ref, o_ref):
    part = _dot(x_ref[...], w_ref[...])

    @pl.when(pl.program_id(1) == 0)
    def _():
        o_ref[...] = part

    @pl.when(pl.program_id(1) > 0)
    def _():
        o_ref[...] += part


def _mm_kacc_call(x, w, name):
    m, k = x.shape
    n = w.shape[1]
    tm = min(512, m)
    tk = min(MM_TILE, k)
    return pl.pallas_call(
        _mm_kacc_kernel,
        grid=(m // tm, k // tk),
        in_specs=[pl.BlockSpec((tm, tk), lambda i, kk: (i, kk)),
                  pl.BlockSpec((tk, n), lambda i, kk: (kk, 0))],
        out_specs=pl.BlockSpec((tm, n), lambda i, kk: (i, 0)),
        out_shape=jax.ShapeDtypeStruct((m, n), F32),
        compiler_params=_cparams("parallel", "arbitrary"),
        name=name,
    )(x, w)


def _s5_disc_kernel(lr_ref, li_ref, ls_ref, brt_ref, bit_ref, pw_re_ref, pw_im_ref, bb_re_ref, bb_im_ref):
    for d in range(N_DIR):
        lr = lr_ref[d]
        li = li_ref[d]
        delta = jnp.exp(ls_ref[d])
        for k in range(1, SUBLANES + 1):
            mag = jnp.exp(lr * delta * float(k))
            th = li * delta * float(k)
            pw_re_ref[d, k - 1:k, :] = mag * jnp.cos(th)
            pw_im_ref[d, k - 1:k, :] = mag * jnp.sin(th)
        mag = jnp.exp(lr * delta)
        nr = mag * jnp.cos(li * delta) - 1.0
        ni = mag * jnp.sin(li * delta)
        den = lr * lr + li * li
        fr = (nr * lr + ni * li) / den
        fi = (ni * lr - nr * li) / den
        br = brt_ref[d]
        bi = bit_ref[d]
        bb_re_ref[d] = fr * br - fi * bi
        bb_im_ref[d] = fr * bi + fi * br


def _s5_params(lam_re, lam_im, log_step, b_re, b_im, c_re, c_im):
    nd, g, p = lam_re.shape
    gp = g * p
    nb = g // S5_BLOCK_GROUPS
    ls = jnp.repeat(log_step, p, axis=-1).reshape(nd, 1, gp)
    brt = b_re.transpose(0, 3, 1, 2).reshape(nd, S5_GROUP_CH, gp)
    bit = b_im.transpose(0, 3, 1, 2).reshape(nd, S5_GROUP_CH, gp)
    full = lambda *shape: pl.BlockSpec(shape, lambda: (0,) * len(shape))
    pw_re, pw_im, bb_re, bb_im = pl.pallas_call(
        _s5_disc_kernel,
        in_specs=[full(nd, 1, gp), full(nd, 1, gp), full(nd, 1, gp),
                  full(nd, S5_GROUP_CH, gp), full(nd, S5_GROUP_CH, gp)],
        out_specs=[full(nd, SUBLANES, gp), full(nd, SUBLANES, gp),
                   full(nd, S5_GROUP_CH, gp), full(nd, S5_GROUP_CH, gp)],
        out_shape=[jax.ShapeDtypeStruct((nd, SUBLANES, gp), F32)] * 2
        + [jax.ShapeDtypeStruct((nd, S5_GROUP_CH, gp), F32)] * 2,
        name="s5_discretize",
    )(lam_re.reshape(nd, 1, gp), lam_im.reshape(nd, 1, gp), ls, brt, bit)

    eye = jnp.eye(S5_BLOCK_GROUPS, dtype=F32)

    def in_proj(bb):
        bb = bb.reshape(nd, S5_GROUP_CH, nb, S5_BLOCK_GROUPS, p)
        w = jnp.einsum('dhjgp,gk->djghkp', bb, eye)
        return w.reshape(nd, nb, LANES, S5_BLOCK_STATES)

    def out_proj(c):
        c = c.reshape(nd, nb, S5_BLOCK_GROUPS, S5_GROUP_CH, p)
        w = jnp.einsum('djghp,gk->djgpkh', c, eye)
        return w.reshape(nd, nb, S5_BLOCK_STATES, LANES)

    wb = jnp.concatenate([in_proj(bb_re), in_proj(bb_im)], axis=-1).astype(BF16)
    wc = jnp.concatenate([out_proj(c_re), -out_proj(c_im)], axis=-2).astype(BF16)

    rows = jnp.arange(SUBLANES)
    coefs = []
    for d in range(nd):
        rev = d == 1
        kinds = []
        for sh in (1, 2, 4):
            keep = (rows < SUBLANES - sh) if rev else (rows >= sh)
            for pw in (pw_re, pw_im):
                kinds.append(jnp.where(keep[:, None], pw[d, sh - 1][None, :], 0.0))
        order = (SUBLANES - 1 - rows) if rev else rows
        for pw in (pw_re, pw_im):
            kinds.append(pw[d][order])
        coefs.append(jnp.stack(kinds))
    coef = jnp.stack(coefs).reshape(nd, 8, SUBLANES, nb, S5_BLOCK_STATES).transpose(0, 3, 1, 2, 4)
    return wb, wc, coef


def _s5_scan_kernel(uf_ref, ub_ref, wb_ref, wc_ref, coef_ref, yf_ref, yb_ref, bu_ref, carry_ref,
                    *, tt, nb, sq):
    i = pl.program_id(0)
    n = pl.num_programs(0)
    ns = S5_BLOCK_STATES
    ngroups = tt // SUBLANES
    for d, (u_ref, y_ref) in enumerate(((uf_ref, yf_ref), (ub_ref, yb_ref))):
        rev = d == 1
        ti = (n - 1 - i) if rev else i
        _, first, last = _tile_seq(ti, tt, sq)

        @pl.when(last if rev else first)
        def _():
            carry_ref[d] = jnp.zeros(carry_ref.shape[1:], F32)

        for j in range(nb):
            ub = u_ref[:, LANES * j:LANES * (j + 1)].astype(BF16)
            bu_ref[d] = _dot(ub, wb_ref[d, j])

            def body(r, c, d=d, j=j, rev=rev):
                c_re, c_im = c
                rr = (ngroups - 1 - r) if rev else r
                row = pl.multiple_of(rr * SUBLANES, SUBLANES)
                x = bu_ref[d, pl.ds(row, SUBLANES), :]
                xr, xi = x[:, :ns], x[:, ns:]
                for lvl, sh in enumerate((1, 2, 4)):
                    ar = coef_ref[d, j, 2 * lvl]
                    ai = coef_ref[d, j, 2 * lvl + 1]
                    s = (SUBLANES - sh) if rev else sh
                    yr = pltpu.roll(xr, s, 0)
                    yi = pltpu.roll(xi, s, 0)
                    xr, xi = xr + ar * yr - ai * yi, xi + ar * yi + ai * yr
                pr = coef_ref[d, j, 6]
                pi = coef_ref[d, j, 7]
                sr = xr + pr * c_re - pi * c_im
                si = xi + pr * c_im + pi * c_re
                bu_ref[d, pl.ds(row, SUBLANES), :] = jnp.concatenate([sr, si], axis=1)
                e = 0 if rev else SUBLANES - 1
                return (jnp.broadcast_to(sr[e:e + 1, :], sr.shape),
                        jnp.broadcast_to(si[e:e + 1, :], si.shape))

            c0 = carry_ref[d, j]
            c_re, c_im = lax.fori_loop(0, ngroups, body, (c0[:, :ns], c0[:, ns:]))
            carry_ref[d, j] = jnp.concatenate([c_re, c_im], axis=1)
            y_ref[:, LANES * j:LANES * (j + 1)] = _dot(bu_ref[d].astype(BF16), wc_ref[d, j])


def _s5_scan_call(proj, dm, sq, wb, wc, coef):
    t = proj.shape[0]
    tt = min(256, sq.l1, sq.l2)
    n = t // tt
    nb = dm.s5w // LANES
    cb = dm.o_s5 // dm.s5w
    full = lambda a: pl.BlockSpec(a.shape, lambda i: (0,) * a.ndim)
    return pl.pallas_call(
        functools.partial(_s5_scan_kernel, tt=tt, nb=nb, sq=sq),
        grid=(n,),
        in_specs=[pl.BlockSpec((tt, dm.s5w), lambda i: (i, cb)),
                  pl.BlockSpec((tt, dm.s5w), lambda i: (n - 1 - i, cb)),
                  full(wb), full(wc), full(coef)],
        out_specs=[pl.BlockSpec((tt, dm.s5w), lambda i: (i, 0)),
                   pl.BlockSpec((tt, dm.s5w), lambda i: (n - 1 - i, 0))],
        out_shape=[jax.ShapeDtypeStruct((t, dm.s5w), F32)] * 2,
        scratch_shapes=[pltpu.VMEM((N_DIR, tt, 2 * S5_BLOCK_STATES), F32),
                        pltpu.VMEM((N_DIR, nb, SUBLANES, 2 * S5_BLOCK_STATES), F32)],
        compiler_params=_cparams("arbitrary"),
        name="s5_scan",
    )(proj, proj, wb, wc, coef)


def _s5_out_kernel(u_ref, yf_ref, yb_ref, d_ref, w_ref, b_ref, o_ref):
    y = u_ref[...] * d_ref[...] + yf_ref[...] + yb_ref[...]
    y = jax.nn.gelu(y)
    gate = jax.nn.sigmoid(_dot(y.astype(BF16), w_ref[...]) + b_ref[...])
    o_ref[...] = (y * gate).astype(BF16)


def _s5_out_call(proj, yf, yb, dm, d_skip, glu_w, glu_b):
    t = proj.shape[0]
    w = dm.s5w
    tm = min(512, t)
    row = lambda c: pl.BlockSpec((tm, w), lambda i: (i, c))
    vec = pl.BlockSpec((1, w), lambda i: (0, 0))
    return pl.pallas_call(
        _s5_out_kernel,
        grid=(t // tm,),
        in_specs=[row(dm.o_s5 // w), row(0), row(0), vec, pl.BlockSpec((w, w), lambda i: (0, 0)), vec],
        out_specs=row(0),
        out_shape=jax.ShapeDtypeStruct((t, w), BF16),
        compiler_params=_cparams("parallel"),
        name="s5_out",
    )(proj, yf, yb, d_skip.reshape(1, w), glu_w.astype(BF16), glu_b.reshape(1, w))


def _gla_kernel(qf_ref, kf_ref, vf_ref, sf_ref, qb_ref, kb_ref, vb_ref, sb_ref, gu_ref, bias_ref,
                of_ref, ob_ref, st_ref, *, nc, npairs, tb, sq):
    i = pl.program_id(0)
    n = pl.num_programs(0)
    row = lax.broadcasted_iota(jnp.int32, (CHUNK, CHUNK), 0)
    col = lax.broadcasted_iota(jnp.int32, (CHUNK, CHUNK), 1)
    lane = lax.broadcasted_iota(jnp.int32, (CHUNK, LANES), 1)
    first_half = lane < GLA_DK
    scale = GLA_DK ** -0.5
    dirs = ((qf_ref, kf_ref, vf_ref, sf_ref, of_ref), (qb_ref, kb_ref, vb_ref, sb_ref, ob_ref))
    for d, (q_ref, k_ref, v_ref, s_ref, o_ref) in enumerate(dirs):
        rev = d == 1
        ti = (n - 1 - i) if rev else i
        _, first, last = _tile_seq(ti, tb, sq)

        @pl.when(last if rev else first)
        def _():
            st_ref[d] = jnp.zeros(st_ref.shape[1:], F32)

        tri = jnp.where((col >= row) if rev else (col <= row), 1.0, 0.0)
        keep = (col > row) if rev else (col <= row)
        for cc in range(nc):
            c = (nc - 1 - cc) if rev else cc
            rows = slice(CHUNK * c, CHUNK * (c + 1))
            z = _dot(s_ref[rows, :], gu_ref[d]) + bias_ref[d]
            log_a = _log_sigmoid(z) * (1.0 / GLA_GATE_NORM)
            b = _dot(tri, log_a, precision=HIGHEST)
            b_last = b[0:1, :] if rev else b[CHUNK - 1:CHUNK, :]
            q = q_ref[rows, :] * scale
            k = k_ref[rows, :]
            q_dec = q * jnp.exp(b)
            k_inv = k * jnp.exp(-b)
            k_dec = k * jnp.exp(b_last - b)
            dec = jnp.exp(b_last)
            for p in range(npairs):
                sl = slice(LANES * p, LANES * (p + 1))
                qp, kip, kdp = q_dec[:, sl], k_inv[:, sl], k_dec[:, sl]
                st = st_ref[d, p]
                new = st * dec[:, sl]
                for half in range(2):
                    h = 2 * p + half
                    msk = first_half if half == 0 else jnp.logical_not(first_half)
                    qm = jnp.where(msk, qp, 0.0)
                    attn = jnp.where(keep, _dot_nt(qm, kip), 0.0)
                    vh = v_ref[rows, GLA_DV * h:GLA_DV * (h + 1)]
                    o_ref[rows, GLA_DV * h:GLA_DV * (h + 1)] = _dot(attn, vh) + _dot_nt(qm, st)
                    new = new + _dot_tn(vh, jnp.where(msk, kdp, 0.0))
                st_ref[d, p] = new


def _gla_call(proj, dm, sq, gate_up, gate_bias):
    t = proj.shape[0]
    nc = 4
    tb = min(nc * CHUNK, sq.l1, sq.l2)
    nc = tb // CHUNK
    n = t // tb
    npairs = dm.gla_h // 2
    qk_w = dm.gla_h * GLA_DK
    gu = jnp.zeros((N_DIR, LANES, qk_w), F32)
    for d in range(N_DIR):
        gu = gu.at[d, d * GLA_LOW_RANK:(d + 1) * GLA_LOW_RANK, :].set(gate_up[d])
    bias = gate_bias.reshape(N_DIR, 1, qk_w)

    def specs(tile):
        return [pl.BlockSpec((tb, qk_w), lambda i: (tile(i), dm.o_gq // qk_w)),
                pl.BlockSpec((tb, qk_w), lambda i: (tile(i), dm.o_gk // qk_w)),
                pl.BlockSpec((tb, dm.gla_w), lambda i: (tile(i), dm.o_gv // dm.gla_w)),
                pl.BlockSpec((tb, LANES), lambda i: (tile(i), dm.o_small // LANES))]

    fwd = lambda i: i
    bwd = lambda i: n - 1 - i
    full = lambda a: pl.BlockSpec(a.shape, lambda i: (0,) * a.ndim)
    return pl.pallas_call(
        functools.partial(_gla_kernel, nc=nc, npairs=npairs, tb=tb, sq=sq),
        grid=(n,),
        in_specs=specs(fwd) + specs(bwd) + [full(gu), full(bias)],
        out_specs=[pl.BlockSpec((tb, dm.gla_w), lambda i: (i, 0)),
                   pl.BlockSpec((tb, dm.gla_w), lambda i: (n - 1 - i, 0))],
        out_shape=[jax.ShapeDtypeStruct((t, dm.gla_w), F32)] * 2,
        scratch_shapes=[pltpu.VMEM((N_DIR, npairs, GLA_DV, LANES), F32)],
        compiler_params=_cparams("arbitrary"),
        name="gla_chunks",
    )(*([proj] * 8), gu, bias)


def _gla_out_kernel(of_ref, ob_ref, g_ref, gain_ref, o_ref, *, heads):
    for h in range(heads):
        sl = slice(GLA_DV * h, GLA_DV * (h + 1))
        o = of_ref[:, sl] + ob_ref[:, sl]
        r = lax.rsqrt(jnp.mean(o * o, axis=-1, keepdims=True) + NORM_EPS)
        o_ref[:, sl] = (o * r * gain_ref[...] * _silu(g_ref[:, sl])).astype(BF16)


def _gla_out_call(proj, of, ob, dm, gain):
    t = proj.shape[0]
    w = dm.gla_w
    tm = min(512, t)
    row = lambda c: pl.BlockSpec((tm, w), lambda i: (i, c))
    return pl.pallas_call(
        functools.partial(_gla_out_kernel, heads=dm.gla_h),
        grid=(t // tm,),
        in_specs=[row(0), row(0), row(dm.o_gg // w), pl.BlockSpec((1, GLA_DV), lambda i: (0, 0))],
        out_specs=row(0),
        out_shape=jax.ShapeDtypeStruct((t, w), BF16),
        compiler_params=_cparams("parallel"),
        name="gla_out",
    )(of, ob, proj, gain.reshape(1, GLA_DV))


GDN_BETA_LANE = N_DIR * GLA_LOW_RANK


def _split_bf16(x):
    hi = x.astype(BF16)
    return hi, (x - hi.astype(F32)).astype(BF16)


def _dot3(a, b):
    (ah, al), (bh, bl) = a, b
    return _dot(ah, bh) + _dot(ah, bl) + _dot(al, bh)


def _unit_triangular_inverse(a, row, col):
    inv = None
    b = 1
    while b < CHUNK:
        pair = jnp.logical_and((row // (2 * b)) == (col // (2 * b)), (row // b) != (col // b))
        off = jnp.where(pair, a, 0.0)
        if inv is None:
            inv = jnp.where(row == col, 1.0, 0.0) - off
        else:
            inv_s = _split_bf16(inv)
            t = _dot3(inv_s, _split_bf16(off))
            inv = inv - _dot3(_split_bf16(t), inv_s)
        b *= 2
    return inv


def _gdn_wy_kernel(x_ref, prev_ref, next_ref, s_ref, cw_ref, alog_ref, dtb_ref,
                   u_ref, w_ref, qd_ref, kd_ref, qk_ref, gc_ref,
                   ext_ref, q_ref, k_ref, v_ref, beta_ref, gcx_ref, gct_ref, *, tm, heads, sq):
    i = pl.program_id(0)
    _, first, last = _tile_seq(i, tm, sq)
    halo = SUBLANES
    pad = GDN_CONV // 2
    w = heads * GDN_DK
    ext_ref[0:halo, :] = jnp.where(first, 0.0, prev_ref[...])
    ext_ref[halo:halo + tm, :] = x_ref[...]
    ext_ref[halo + tm:, :] = jnp.where(last, 0.0, next_ref[...])
    acc = cw_ref[0:1, :] * ext_ref[pl.ds(halo - pad, tm), :]
    for j in range(1, GDN_CONV):
        acc = acc + cw_ref[j:j + 1, :] * ext_ref[pl.ds(halo - pad + j, tm), :]
    y = _silu(acc)
    for h in range(heads):
        qh = y[:, GDN_DK * h:GDN_DK * (h + 1)]
        kh = y[:, w + GDN_DK * h:w + GDN_DK * (h + 1)]
        q_ref[h] = qh * lax.rsqrt(jnp.sum(qh * qh, axis=-1, keepdims=True) + NORM_EPS) * (GDN_DK ** -0.5)
        k_ref[h] = kh * lax.rsqrt(jnp.sum(kh * kh, axis=-1, keepdims=True) + NORM_EPS)
        v_ref[h] = y[:, 2 * w + GDN_DV * h:2 * w + GDN_DV * (h + 1)]
    sm = s_ref[...]
    beta = jax.nn.sigmoid(sm)
    g = -jnp.exp(alog_ref[...]) * _softplus(sm + dtb_ref[...])
    row = lax.broadcasted_iota(jnp.int32, (tm, tm), 0)
    col = lax.broadcasted_iota(jnp.int32, (tm, tm), 1)
    same = (row // CHUNK) == (col // CHUNK)
    a_lane = GDN_BETA_LANE + N_DIR * heads
    for d in range(N_DIR):
        tri = jnp.where(jnp.logical_and(same, (col >= row) if d == 1 else (col <= row)), 1.0, 0.0)
        gc = _dot(tri, g, precision=HIGHEST)
        gc_ref[d] = gc
        for h in range(heads):
            lb = GDN_BETA_LANE + d * heads + h
            la = a_lane + d * heads + h
            beta_ref[d, h] = jnp.broadcast_to(beta[:, lb:lb + 1], (tm, LANES))
            gcx_ref[d, h] = jnp.broadcast_to(gc[:, la:la + 1], (tm, LANES))
        for c in range(tm // CHUNK):
            gcm = gc[CHUNK * c:CHUNK * (c + 1), :]
            gct_ref[d, c] = jnp.concatenate([gcm, jnp.zeros_like(gcm)], axis=0).T

    r64 = lax.broadcasted_iota(jnp.int32, (CHUNK, CHUNK), 0)
    c64 = lax.broadcasted_iota(jnp.int32, (CHUNK, CHUNK), 1)
    zeros = jnp.zeros((CHUNK, LANES - CHUNK), F32)

    def body(h, carry):
        for c in range(tm // CHUNK):
            rows = pl.ds(CHUNK * c, CHUNK)
            q = q_ref[h, rows, :]
            k = k_ref[h, rows, :]
            v = v_ref[h, rows, :]
            gram = _dot_nt(k, k)
            qkt = _dot_nt(q, k)
            for d in range(N_DIR):
                rev = d == 1
                incl = (c64 >= r64) if rev else (c64 <= r64)
                strict = (c64 > r64) if rev else (c64 < r64)
                bt = beta_ref[d, h, rows, :]
                gc = gcx_ref[d, h, rows, :]
                gc_row = gct_ref[d, c, pl.ds(a_lane + d * heads + h, 1), :][:, :CHUNK]
                decay = jnp.exp(jnp.where(incl, gc[:, :CHUNK] - gc_row, -jnp.inf))
                a = jnp.where(strict, bt[:, :CHUNK] * gram * decay, 0.0)
                inv = _unit_triangular_inverse(a, r64, c64)
                kb = k * bt
                rhs = jnp.concatenate([v * bt, kb * jnp.exp(gc)], axis=1)
                uw = _dot3(_split_bf16(inv), _split_bf16(rhs))
                u_ref[d, h, rows, :] = uw[:, :GDN_DV]
                w_ref[d, h, rows, :] = uw[:, GDN_DV:].astype(BF16)
                e = 0 if rev else CHUNK - 1
                qd_ref[d, h, rows, :] = (q * jnp.exp(gc)).astype(BF16)
                kd_ref[d, h, rows, :] = (k * jnp.exp(gc[e:e + 1, :] - gc)).astype(BF16)
                qk = jnp.where(strict if rev else incl, qkt * decay, 0.0)
                qk_ref[d, h, rows, :] = jnp.concatenate([qk, zeros], axis=1).astype(BF16)
        return carry

    lax.fori_loop(0, heads, body, 0)


def _gdn_wy_call(proj, dm, sq, conv_w, a_log, dt_bias):
    t = proj.shape[0]
    heads = dm.gdn_h
    tm = min(2 * CHUNK, sq.l1, sq.l2)
    n = t // tm
    wq = 3 * dm.gdn_w
    hb = tm // SUBLANES
    a_lane = GDN_BETA_LANE + N_DIR * heads
    alog = jnp.zeros((1, LANES), F32).at[0, a_lane:a_lane + N_DIR * heads].set(a_log.reshape(-1))
    dtb = jnp.zeros((1, LANES), F32).at[0, a_lane:a_lane + N_DIR * heads].set(dt_bias.reshape(-1))
    cw = conv_w.reshape(GDN_CONV, wq)
    cb = dm.o_qkv // wq
    nhb = t // SUBLANES
    dhm = lambda dt: jax.ShapeDtypeStruct((N_DIR, heads, t, LANES), dt)
    dhm_spec = pl.BlockSpec((N_DIR, heads, tm, LANES), lambda i: (0, 0, i, 0))
    return pl.pallas_call(
        functools.partial(_gdn_wy_kernel, tm=tm, heads=heads, sq=sq),
        grid=(n,),
        in_specs=[pl.BlockSpec((tm, wq), lambda i: (i, cb)),
                  pl.BlockSpec((SUBLANES, wq), lambda i: (jnp.maximum(i * hb - 1, 0), cb)),
                  pl.BlockSpec((SUBLANES, wq), lambda i: (jnp.minimum((i + 1) * hb, nhb - 1), cb)),
                  pl.BlockSpec((tm, LANES), lambda i: (i, dm.o_small // LANES)),
                  pl.BlockSpec((GDN_CONV, wq), lambda i: (0, 0)),
                  pl.BlockSpec((1, LANES), lambda i: (0, 0)),
                  pl.BlockSpec((1, LANES), lambda i: (0, 0))],
        out_specs=[dhm_spec] * 5 + [pl.BlockSpec((N_DIR, tm, LANES), lambda i: (0, i, 0))],
        out_shape=[dhm(F32)] + [dhm(BF16)] * 4 + [jax.ShapeDtypeStruct((N_DIR, t, LANES), F32)],
        scratch_shapes=[pltpu.VMEM((tm + 2 * SUBLANES, wq), F32)]
        + [pltpu.VMEM((heads, tm, LANES), F32)] * 3
        + [pltpu.VMEM((N_DIR, heads, tm, LANES), F32)] * 2
        + [pltpu.VMEM((N_DIR, tm // CHUNK, LANES, LANES), F32)],
        compiler_params=_cparams("parallel"),
        name="gdn_wy",
    )(proj, proj, proj, proj, cw, alog, dtb)


def _gdn_scan_kernel(uf_ref, wf_ref, qdf_ref, kdf_ref, qkf_ref, gcf_ref,
                     ub_ref, wb_ref, qdb_ref, kdb_ref, qkb_ref, gcb_ref,
                     of_ref, ob_ref, s_ref, *, nc, heads, tb, sq):
    i = pl.program_id(0)
    n = pl.num_programs(0)
    a_lane = GDN_BETA_LANE + N_DIR * heads
    dirs = ((uf_ref, wf_ref, qdf_ref, kdf_ref, qkf_ref, gcf_ref, of_ref),
            (ub_ref, wb_ref, qdb_ref, kdb_ref, qkb_ref, gcb_ref, ob_ref))
    for d in range(N_DIR):
        rev = d == 1
        ti = (n - 1 - i) if rev else i
        _, first, last = _tile_seq(ti, tb, sq)

        @pl.when(last if rev else first)
        def _():
            s_ref[d] = jnp.zeros(s_ref.shape[1:], F32)

    for cc in range(nc):
        for d, (u_ref, w_ref, qd_ref, kd_ref, qk_ref, gc_ref, o_ref) in enumerate(dirs):
            rev = d == 1
            c = (nc - 1 - cc) if rev else cc
            rows = slice(CHUNK * c, CHUNK * (c + 1))
            e = CHUNK * c + (0 if rev else CHUNK - 1)
            for h in range(heads):
                state = s_ref[d, h]
                wq = _dot(jnp.concatenate([w_ref[0, h, rows, :], qd_ref[0, h, rows, :]], axis=0),
                          state.astype(BF16))
                v_new = (u_ref[0, h, rows, :] - wq[:CHUNK]).astype(BF16)
                o_ref[h, rows, :] = wq[CHUNK:] + _dot(qk_ref[0, h, rows, :CHUNK], v_new)
                la = a_lane + d * heads + h
                dec = jnp.exp(gc_ref[0, e:e + 1, la:la + 1])
                s_ref[d, h] = state * dec + _dot_tn(kd_ref[0, h, rows, :], v_new)


def _gdn_scan_call(u, w, qd, kd, qk, gc, dm, sq):
    _, heads, t, _ = u.shape
    nc = 2
    tb = min(nc * CHUNK, sq.l1, sq.l2)
    nc = tb // CHUNK
    n = t // tb
    fwd = lambda i: i
    bwd = lambda i: n - 1 - i
    hm = lambda tile: pl.BlockSpec((heads, tb, LANES), lambda i: (0, tile(i), 0))
    dhm = lambda d, tile: pl.BlockSpec((1, heads, tb, LANES), lambda i: (d, 0, tile(i), 0))
    cm = lambda d, tile: pl.BlockSpec((1, tb, LANES), lambda i: (d, tile(i), 0))
    per_dir = lambda d, tile: [dhm(d, tile)] * 5 + [cm(d, tile)]
    return pl.pallas_call(
        functools.partial(_gdn_scan_kernel, nc=nc, heads=heads, tb=tb, sq=sq),
        grid=(n,),
        in_specs=per_dir(0, fwd) + per_dir(1, bwd),
        out_specs=[hm(fwd), hm(bwd)],
        out_shape=[jax.ShapeDtypeStruct((heads, t, LANES), F32)] * 2,
        scratch_shapes=[pltpu.VMEM((N_DIR, heads, GDN_DK, GDN_DV), F32)],
        compiler_params=_cparams("arbitrary"),
        name="gdn_scan",
    )(u, w, qd, kd, qk, gc, u, w, qd, kd, qk, gc)


def _gdn_out_kernel(of_ref, ob_ref, z_ref, gain_ref, o_ref, *, heads):
    for h in range(heads):
        sl = slice(GDN_DV * h, GDN_DV * (h + 1))
        o = of_ref[h] + ob_ref[h]
        r = lax.rsqrt(jnp.mean(o * o, axis=-1, keepdims=True) + NORM_EPS)
        o_ref[:, sl] = (o * r * gain_ref[...] * _silu(z_ref[:, sl])).astype(BF16)


def _gdn_out_call(proj, of, ob, dm, gain):
    t = proj.shape[0]
    w = dm.gdn_w
    heads = dm.gdn_h
    tm = min(512, t)
    hm = pl.BlockSpec((heads, tm, LANES), lambda i: (0, i, 0))
    return pl.pallas_call(
        functools.partial(_gdn_out_kernel, heads=heads),
        grid=(t // tm,),
        in_specs=[hm, hm, pl.BlockSpec((tm, w), lambda i: (i, dm.o_z // w)),
                  pl.BlockSpec((1, GDN_DV), lambda i: (0, 0))],
        out_specs=pl.BlockSpec((tm, w), lambda i: (i, 0)),
        out_shape=jax.ShapeDtypeStruct((t, w), BF16),
        compiler_params=_cparams("parallel"),
        name="gdn_out",
    )(of, ob, proj, gain.reshape(1, GDN_DV))


def _ffn_act_kernel(g_ref, gp_ref, gn_ref, v_ref, vp_ref, vn_ref, cwg_ref, cwv_ref, o_ref, ext_ref,
                    *, tm, sq):
    i = pl.program_id(0)
    _, first, last = _tile_seq(i, tm, sq)
    halo = SUBLANES
    pad = FFN_CONV // 2

    def conv(x_ref, prev_ref, next_ref, cw_ref):
        ext_ref[0:halo, :] = jnp.where(first, 0.0, prev_ref[...])
        ext_ref[halo:halo + tm, :] = x_ref[...]
        ext_ref[halo + tm:, :] = jnp.where(last, 0.0, next_ref[...])
        acc = cw_ref[0:1, :] * ext_ref[pl.ds(halo - pad, tm), :]
        for j in range(1, FFN_CONV):
            acc = acc + cw_ref[j:j + 1, :] * ext_ref[pl.ds(halo - pad + j, tm), :]
        return acc

    gate = conv(g_ref, gp_ref, gn_ref, cwg_ref)
    val = conv(v_ref, vp_ref, vn_ref, cwv_ref)
    o_ref[...] = (_silu(gate) * val).astype(BF16)


def _ffn_act_call(h1, dm, sq, cw_gate, cw_val):
    t = h1.shape[0]
    f = dm.dffp
    tm = min(256, sq.l1, sq.l2)
    tc = min(MM_TILE, f)
    hb = tm // SUBLANES
    nhb = t // SUBLANES
    nv = f // tc
    main = lambda off: pl.BlockSpec((tm, tc), lambda i, j: (i, j + off))
    prev = lambda off: pl.BlockSpec((SUBLANES, tc), lambda i, j: (jnp.maximum(i * hb - 1, 0), j + off))
    nxt = lambda off: pl.BlockSpec((SUBLANES, tc), lambda i, j: (jnp.minimum((i + 1) * hb, nhb - 1), j + off))
    cws = pl.BlockSpec((FFN_CONV, tc), lambda i, j: (0, j))
    return pl.pallas_call(
        functools.partial(_ffn_act_kernel, tm=tm, sq=sq),
        grid=(t // tm, f // tc),
        in_specs=[main(0), prev(0), nxt(0), main(nv), prev(nv), nxt(nv), cws, cws],
        out_specs=pl.BlockSpec((tm, tc), lambda i, j: (i, j)),
        out_shape=jax.ShapeDtypeStruct((t, f), BF16),
        scratch_shapes=[pltpu.VMEM((tm + 2 * SUBLANES, tc), F32)],
        compiler_params=_cparams("parallel", "parallel"),
        name="ffn_conv_swiglu",
    )(h1, h1, h1, h1, h1, h1, cw_gate, cw_val)


def _pad_cols(w, n):
    return jnp.pad(w, ((0, 0), (0, n - w.shape[1])))


def _layout_w_in(w_in, dm):
    d = dm.d
    qk = dm.gla_h * GLA_DK
    widths = (dm.s5w, qk, qk, dm.gla_w, dm.gla_w, N_DIR * GLA_LOW_RANK, 3 * dm.gdn_w, dm.gdn_w,
              N_DIR * dm.gdn_h, N_DIR * dm.gdn_h)
    splits = np.cumsum(widths)[:-1]
    s5_u, gq, gk, gv, gg, glr, qkv, z, beta, a = jnp.split(w_in, splits, axis=1)
    small = _pad_cols(jnp.concatenate([glr, beta, a], axis=1), LANES)
    out = jnp.zeros((d, dm.n_in_pad), w_in.dtype)
    for off, blk in ((dm.o_qkv, qkv), (dm.o_z, z), (dm.o_gv, gv), (dm.o_gg, gg), (dm.o_gq, gq),
                     (dm.o_gk, gk), (dm.o_small, small), (dm.o_s5, s5_u)):
        out = lax.dynamic_update_slice(out, blk, (0, off))
    return out.astype(BF16)


def _layer(x, sq, dm, mod, norm_gains, w_in, s5, gla, gdn, w_out, ffn_up, ffn_conv, ffn_down,
           pending):
    d = dm.d
    mod3 = mod.reshape(mod.shape[0], 1, 6 * d)
    gains3 = norm_gains.reshape(4, 1, d)
    if pending is None:
        (h,) = _resnorm_call(x, sq, norm=(mod3, 1, 0, gains3, 0))
    else:
        x, h = _resnorm_call(x, sq, res=pending, norm=(mod3, 1, 0, gains3, 0))
    proj = _mm_call(h, _layout_w_in(w_in, dm), F32, "in_proj")

    wb, wc, coef = _s5_params(*s5[:7])
    yf, yb = _s5_scan_call(proj, dm, sq, wb, wc, coef)
    y_s5 = _s5_out_call(proj, yf, yb, dm, s5[7], s5[8], s5[9])

    of, ob = _gla_call(proj, dm, sq, gla[0], gla[1])
    y_gla = _gla_out_call(proj, of, ob, dm, gla[2])

    of, ob = _gdn_scan_call(*_gdn_wy_call(proj, dm, sq, gdn[0], gdn[1], gdn[2]), dm, sq)
    y_gdn = _gdn_out_call(proj, of, ob, dm, gdn[3])

    mixed = _mm_call(jnp.concatenate([y_s5, y_gla, y_gdn], axis=1), w_out.astype(BF16), F32, "out_proj")
    x, h = _resnorm_call(x, sq, res=(mixed, mod3, 2, gains3, 1), norm=(mod3, 4, 3, gains3, 2))
    f, fp = dm.dff, dm.dffp
    up = jnp.concatenate([_pad_cols(ffn_up[:, :f], fp), _pad_cols(ffn_up[:, f:], fp)], axis=1).astype(BF16)
    cw = ffn_conv.reshape(FFN_CONV, 2 * f)
    h1 = _mm_call(h, up, F32, "ffn_up")
    act = _ffn_act_call(h1, dm, sq, _pad_cols(cw[:, :f], fp), _pad_cols(cw[:, f:], fp))
    down = jnp.pad(ffn_down, ((0, fp - f), (0, 0))).astype(BF16)
    f_out = _mm_kacc_call(act, down, "ffn_down")
    return x, (f_out, mod3, 5, gains3, 3)


def kernel(x_prompt, x_sample, c_prompt, c_sample, ada_w, ada_b, norm_gains, w_in, s5_lambda_re, s5_lambda_im, s5_log_step, s5_b_re, s5_b_im, s5_c_re, s5_c_im, s5_d, s5_glu_w, s5_glu_b, gla_gate_up, gla_gate_bias, gla_norm, gdn_conv, gdn_a_log, gdn_dt_bias, gdn_norm, w_out, ffn_up, ffn_conv, ffn_down):
    b1, l1, d = x_prompt.shape
    b2, l2, _ = x_sample.shape
    sq = _Seq(b1, l1, b2, l2)
    dm = _make_dims(d)
    depth = ada_w.shape[0]
    x = jnp.concatenate([x_prompt.reshape(b1 * l1, d), x_sample.reshape(b2 * l2, d)], axis=0)
    c = jnp.concatenate([c_prompt, c_sample], axis=0)
    c = jnp.pad(c, ((0, _round_up(sq.nseq, 2 * SUBLANES) - sq.nseq), (0, 0)))
    mod = _mod_call(c, ada_w, ada_b)
    pending = None
    for l in range(depth):
        s5 = (s5_lambda_re[l], s5_lambda_im[l], s5_log_step[l], s5_b_re[l], s5_b_im[l], s5_c_re[l],
              s5_c_im[l], s5_d[l], s5_glu_w[l], s5_glu_b[l])
        gla = (gla_gate_up[l], gla_gate_bias[l], gla_norm[l])
        gdn = (gdn_conv[l], gdn_a_log[l], gdn_dt_bias[l], gdn_norm[l])
        x, pending = _layer(x, sq, dm, mod[l], norm_gains[l], w_in[l], s5, gla, gdn, w_out[l],
                            ffn_up[l], ffn_conv[l], ffn_down[l], pending)
    (x,) = _resnorm_call(x, sq, res=pending)
    return (x[:b1 * l1].reshape(b1, l1, d), x[b1 * l1:].reshape(b2, l2, d))
```

```python
import functools
import math
from typing import NamedTuple

import jax
import jax.numpy as jnp
import numpy as np
from jax import lax
from jax.experimental import pallas as pl
from jax.experimental.pallas import tpu as pltpu

F32 = jnp.float32
BF16 = jnp.bfloat16
HIGHEST = lax.Precision.HIGHEST

NORM_EPS = 1e-6
CHUNK = 64
LANES = 128
SUBLANES = 8
BF16_ROWS = 16
N_DIR = 2
S5_GROUP_CH = 16
S5_STATE = 64
S5_BLOCK_GROUPS = LANES // S5_GROUP_CH
S5_BLOCK_STATES = S5_BLOCK_GROUPS * S5_STATE
GLA_DV = 128
GLA_DK = 64
GLA_LOW_RANK = 16
GLA_GATE_NORM = 16.0
GDN_DK = 128
GDN_DV = 128
GDN_CONV = 5
FFN_CONV = 3
VMEM_LIMIT_BYTES = 56 * 1024 * 1024
MM_TILE = 1024


class _Seq(NamedTuple):
    b1: int
    l1: int
    b2: int
    l2: int

    @property
    def tokens(self):
        return self.b1 * self.l1 + self.b2 * self.l2

    @property
    def nseq(self):
        return self.b1 + self.b2


class _Dims(NamedTuple):
    d: int
    s5w: int
    gla_w: int
    gla_h: int
    gdn_w: int
    gdn_h: int
    dff: int
    dffp: int
    o_qkv: int
    o_z: int
    o_gv: int
    o_gg: int
    o_gq: int
    o_gk: int
    o_small: int
    o_s5: int
    n_in_pad: int


def _round_up(x, m):
    return (x + m - 1) // m * m


def _make_dims(d):
    s5w = d // 4
    gla_w = 3 * d // 8
    gla_h = gla_w // GLA_DV
    gdn_w = d - s5w - gla_w
    gdn_h = gdn_w // GDN_DV
    dff = 256 * ((8 * d // 3 + 255) // 256)
    dffp = _round_up(dff, MM_TILE)
    segs = [3 * gdn_w, gdn_w, gla_w, gla_w, gla_h * GLA_DK, gla_h * GLA_DK, LANES, s5w]
    offs = []
    cur = 0
    for w in segs:
        cur = _round_up(cur, w)
        offs.append(cur)
        cur += w
    return _Dims(d, s5w, gla_w, gla_h, gdn_w, gdn_h, dff, dffp, *offs, _round_up(cur, MM_TILE))


def _tile_seq(i, tm, sq):
    p1 = sq.l1 // tm
    p2 = sq.l2 // tm
    n1 = sq.b1 * p1
    in1 = i < n1
    j = jnp.maximum(i - n1, 0)
    seq = jnp.where(in1, i // p1, sq.b1 + j // p2)
    pos = jnp.where(in1, i % p1, j % p2)
    last = jnp.where(in1, p1 - 1, p2 - 1)
    return seq, pos == 0, pos == last


def _cparams(*sem):
    return pltpu.CompilerParams(dimension_semantics=sem, vmem_limit_bytes=VMEM_LIMIT_BYTES)


def _dot(a, b, **kw):
    return jnp.dot(a, b, preferred_element_type=F32, **kw)


def _dot_nt(a, b, **kw):
    return lax.dot_general(a, b, (((1,), (1,)), ((), ())), preferred_element_type=F32, **kw)


def _dot_tn(a, b, **kw):
    return lax.dot_general(a, b, (((0,), (0,)), ((), ())), preferred_element_type=F32, **kw)


def _softplus(x):
    return jnp.maximum(x, 0.0) + jnp.log1p(jnp.exp(-jnp.abs(x)))


def _log_sigmoid(x):
    return jnp.minimum(x, 0.0) - jnp.log1p(jnp.exp(-jnp.abs(x)))


def _silu(x):
    return x * jax.nn.sigmoid(x)


def _mod_kernel(c_ref, w_ref, b_ref, o_ref):
    a = _silu(c_ref[...]).astype(BF16)
    o_ref[0] = _dot(a, w_ref[0].astype(BF16)) + b_ref[0]


def _mod_call(c_rows, ada_w, ada_b):
    nl, d, n = ada_w.shape
    r = c_rows.shape[0]
    tn = min(512, n)
    return pl.pallas_call(
        _mod_kernel,
        grid=(nl, n // tn),
        in_specs=[pl.BlockSpec((r, d), lambda l, j: (0, 0)),
                  pl.BlockSpec((1, d, tn), lambda l, j: (l, 0, j)),
                  pl.BlockSpec((1, 1, tn), lambda l, j: (l, 0, j))],
        out_specs=pl.BlockSpec((1, r, tn), lambda l, j: (l, 0, j)),
        out_shape=jax.ShapeDtypeStruct((nl, r, n), F32),
        compiler_params=_cparams("parallel", "parallel"),
        name="adaln_mod",
    )(c_rows, ada_w, ada_b.reshape(nl, 1, n))


def _resnorm_kernel(*refs, has_res, has_norm):
    it = iter(refs)
    x_ref = next(it)
    if has_res:
        m_ref, gate_ref, ga_ref = next(it), next(it), next(it)
    if has_norm:
        gb_ref, scale_ref, shift_ref = next(it), next(it), next(it)
    if has_res:
        xo_ref = next(it)
    if has_norm:
        h_ref = next(it)
    x = x_ref[...]
    if has_res:
        m = m_ref[...]
        r = lax.rsqrt(jnp.mean(m * m, axis=-1, keepdims=True) + NORM_EPS)
        x = x + gate_ref[0] * (m * r * ga_ref[0])
        xo_ref[...] = x
    if has_norm:
        r = lax.rsqrt(jnp.mean(x * x, axis=-1, keepdims=True) + NORM_EPS)
        h = (x * r * gb_ref[0]) * (1.0 + scale_ref[0]) + shift_ref[0]
        h_ref[...] = h.astype(BF16)


def _resnorm_call(x, sq, *, res=None, norm=None):
    t, d = x.shape
    tm = min(256, sq.l1, sq.l2)
    row = pl.BlockSpec((tm, d), lambda i: (i, 0))

    def mod_spec(col):
        return pl.BlockSpec((1, 1, d), lambda i: (_tile_seq(i, tm, sq)[0], 0, col))

    def gain_spec(r):
        return pl.BlockSpec((1, 1, d), lambda i: (r, 0, 0))

    args, in_specs, out_shape, out_specs = [x], [row], [], []
    if res is not None:
        m, mod3, gate_col, gains3, gain_row = res
        args += [m, mod3, gains3]
        in_specs += [row, mod_spec(gate_col), gain_spec(gain_row)]
        out_shape.append(jax.ShapeDtypeStruct((t, d), F32))
        out_specs.append(row)
    if norm is not None:
        mod3, scale_col, shift_col, gains3, gain_row = norm
        args += [gains3, mod3, mod3]
        in_specs += [gain_spec(gain_row), mod_spec(scale_col), mod_spec(shift_col)]
        out_shape.append(jax.ShapeDtypeStruct((t, d), BF16))
        out_specs.append(row)
    return pl.pallas_call(
        functools.partial(_resnorm_kernel, has_res=res is not None, has_norm=norm is not None),
        grid=(t // tm,),
        in_specs=in_specs, out_specs=out_specs, out_shape=out_shape,
        compiler_params=_cparams("parallel"),
        name="resnorm",
    )(*args)


def _mm_kernel(x_ref, w_ref, o_ref):
    o_ref[...] = _dot(x_ref[...], w_ref[...]).astype(o_ref.dtype)


def _mm_call(x, w, out_dtype, name):
    m, k = x.shape
    n = w.shape[1]
    tile = MM_TILE if k <= 4 * MM_TILE else MM_TILE // 2
    tm = min(tile, m)
    tn = min(tile, n)
    return pl.pallas_call(
        _mm_kernel,
        grid=(m // tm, n // tn),
        in_specs=[pl.BlockSpec((tm, k), lambda i, j: (i, 0)),
                  pl.BlockSpec((k, tn), lambda i, j: (0, j))],
        out_specs=pl.BlockSpec((tm, tn), lambda i, j: (i, j)),
        out_shape=jax.ShapeDtypeStruct((m, n), out_dtype),
        compiler_params=_cparams("parallel", "arbitrary"),
        name=name,
    )(x, w)


def _s5_disc_kernel(lr_ref, li_ref, ls_ref, brt_ref, bit_ref, pw_re_ref, pw_im_ref, bb_re_ref, bb_im_ref):
    for d in range(N_DIR):
        lr = lr_ref[d]
        li = li_ref[d]
        delta = jnp.exp(ls_ref[d])
        for k in range(1, SUBLANES + 1):
            mag = jnp.exp(lr * delta * float(k))
            th = li * delta * float(k)
            pw_re_ref[d, k - 1:k, :] = mag * jnp.cos(th)
            pw_im_ref[d, k - 1:k, :] = mag * jnp.sin(th)
        mag = jnp.exp(lr * delta)
        nr = mag * jnp.cos(li * delta) - 1.0
        ni = mag * jnp.sin(li * delta)
        den = lr * lr + li * li
        fr = (nr * lr + ni * li) / den
        fi = (ni * lr - nr * li) / den
        br = brt_ref[d]
        bi = bit_ref[d]
        bb_re_ref[d] = fr * br - fi * bi
        bb_im_ref[d] = fr * bi + fi * br


def _s5_params(lam_re, lam_im, log_step, b_re, b_im, c_re, c_im):
    nd, g, p = lam_re.shape
    gp = g * p
    nb = g // S5_BLOCK_GROUPS
    ls = jnp.repeat(log_step, p, axis=-1).reshape(nd, 1, gp)
    brt = b_re.transpose(0, 3, 1, 2).reshape(nd, S5_GROUP_CH, gp)
    bit = b_im.transpose(0, 3, 1, 2).reshape(nd, S5_GROUP_CH, gp)
    full = lambda *shape: pl.BlockSpec(shape, lambda: (0,) * len(shape))
    pw_re, pw_im, bb_re, bb_im = pl.pallas_call(
        _s5_disc_kernel,
        in_specs=[full(nd, 1, gp), full(nd, 1, gp), full(nd, 1, gp),
                  full(nd, S5_GROUP_CH, gp), full(nd, S5_GROUP_CH, gp)],
        out_specs=[full(nd, SUBLANES, gp), full(nd, SUBLANES, gp),
                   full(nd, S5_GROUP_CH, gp), full(nd, S5_GROUP_CH, gp)],
        out_shape=[jax.ShapeDtypeStruct((nd, SUBLANES, gp), F32)] * 2
        + [jax.ShapeDtypeStruct((nd, S5_GROUP_CH, gp), F32)] * 2,
        name="s5_discretize",
    )(lam_re.reshape(nd, 1, gp), lam_im.reshape(nd, 1, gp), ls, brt, bit)

    eye = jnp.eye(S5_BLOCK_GROUPS, dtype=F32)

    def in_proj(bb):
        bb = bb.reshape(nd, S5_GROUP_CH, nb, S5_BLOCK_GROUPS, p)
        w = jnp.einsum('dhjgp,gk->djghkp', bb, eye)
        return w.reshape(nd, nb, LANES, S5_BLOCK_STATES)

    def out_proj(c):
        c = c.reshape(nd, nb, S5_BLOCK_GROUPS, S5_GROUP_CH, p)
        w = jnp.einsum('djghp,gk->djgpkh', c, eye)
        return w.reshape(nd, nb, S5_BLOCK_STATES, LANES)

    wb = jnp.concatenate([in_proj(bb_re), in_proj(bb_im)], axis=-1).astype(BF16)
    wc = jnp.concatenate([out_proj(c_re), -out_proj(c_im)], axis=-2).astype(BF16)

    rows = jnp.arange(SUBLANES)
    coefs = []
    for d in range(nd):
        rev = d == 1
        kinds = []
        for sh in (1, 2, 4):
            keep = (rows < SUBLANES - sh) if rev else (rows >= sh)
            for pw in (pw_re, pw_im):
                kinds.append(jnp.where(keep[:, None], pw[d, sh - 1][None, :], 0.0))
        order = (SUBLANES - 1 - rows) if rev else rows
        for pw in (pw_re, pw_im):
            kinds.append(pw[d][order])
        coefs.append(jnp.stack(kinds))
    coef = jnp.stack(coefs).reshape(nd, 8, SUBLANES, nb, S5_BLOCK_STATES).transpose(0, 3, 1, 2, 4)
    return wb, wc, coef


def _s5_scan_kernel(uf_ref, ub_ref, wb_ref, wc_ref, coef_ref, yf_ref, yb_ref, bu_ref, carry_ref,
                    *, tt, nb, sq):
    i = pl.program_id(0)
    n = pl.num_programs(0)
    ns = S5_BLOCK_STATES
    ngroups = tt // SUBLANES
    for d, (u_ref, y_ref) in enumerate(((uf_ref, yf_ref), (ub_ref, yb_ref))):
        rev = d == 1
        ti = (n - 1 - i) if rev else i
        _, first, last = _tile_seq(ti, tt, sq)

        @pl.when(last if rev else first)
        def _():
            carry_ref[d] = jnp.zeros(carry_ref.shape[1:], F32)

        for j in range(nb):
            ub = u_ref[:, LANES * j:LANES * (j + 1)].astype(BF16)
            bu_ref[d] = _dot(ub, wb_ref[d, j])

            def body(r, c, d=d, j=j, rev=rev):
                c_re, c_im = c
                rr = (ngroups - 1 - r) if rev else r
                row = pl.multiple_of(rr * SUBLANES, SUBLANES)
                x = bu_ref[d, pl.ds(row, SUBLANES), :]
                xr, xi = x[:, :ns], x[:, ns:]
                for lvl, sh in enumerate((1, 2, 4)):
                    ar = coef_ref[d, j, 2 * lvl]
                    ai = coef_ref[d, j, 2 * lvl + 1]
                    s = (SUBLANES - sh) if rev else sh
                    yr = pltpu.roll(xr, s, 0)
                    yi = pltpu.roll(xi, s, 0)
                    xr, xi = xr + ar * yr - ai * yi, xi + ar * yi + ai * yr
                pr = coef_ref[d, j, 6]
                pi = coef_ref[d, j, 7]
                sr = xr + pr * c_re - pi * c_im
                si = xi + pr * c_im + pi * c_re
                bu_ref[d, pl.ds(row, SUBLANES), :] = jnp.concatenate([sr, si], axis=1)
                e = 0 if rev else SUBLANES - 1
                return (jnp.broadcast_to(sr[e:e + 1, :], sr.shape),
                        jnp.broadcast_to(si[e:e + 1, :], si.shape))

            c0 = carry_ref[d, j]
            c_re, c_im = lax.fori_loop(0, ngroups, body, (c0[:, :ns], c0[:, ns:]))
            carry_ref[d, j] = jnp.concatenate([c_re, c_im], axis=1)
            y_ref[:, LANES * j:LANES * (j + 1)] = _dot(bu_ref[d].astype(BF16), wc_ref[d, j])


def _s5_scan_call(proj, dm, sq, wb, wc, coef):
    t = proj.shape[0]
    tt = min(256, sq.l1, sq.l2)
    n = t // tt
    nb = dm.s5w // LANES
    cb = dm.o_s5 // dm.s5w
    full = lambda a: pl.BlockSpec(a.shape, lambda i: (0,) * a.ndim)
    return pl.pallas_call(
        functools.partial(_s5_scan_kernel, tt=tt, nb=nb, sq=sq),
        grid=(n,),
        in_specs=[pl.BlockSpec((tt, dm.s5w), lambda i: (i, cb)),
                  pl.BlockSpec((tt, dm.s5w), lambda i: (n - 1 - i, cb)),
                  full(wb), full(wc), full(coef)],
        out_specs=[pl.BlockSpec((tt, dm.s5w), lambda i: (i, 0)),
                   pl.BlockSpec((tt, dm.s5w), lambda i: (n - 1 - i, 0))],
        out_shape=[jax.ShapeDtypeStruct((t, dm.s5w), F32)] * 2,
        scratch_shapes=[pltpu.VMEM((N_DIR, tt, 2 * S5_BLOCK_STATES), F32),
                        pltpu.VMEM((N_DIR, nb, SUBLANES, 2 * S5_BLOCK_STATES), F32)],
        compiler_params=_cparams("arbitrary"),
        name="s5_scan",
    )(proj, proj, wb, wc, coef)


def _s5_out_kernel(u_ref, yf_ref, yb_ref, d_ref, w_ref, b_ref, o_ref):
    y = u_ref[...] * d_ref[...] + yf_ref[...] + yb_ref[...]
    y = jax.nn.gelu(y)
    gate = jax.nn.sigmoid(_dot(y.astype(BF16), w_ref[...]) + b_ref[...])
    o_ref[...] = (y * gate).astype(BF16)


def _s5_out_call(proj, yf, yb, dm, d_skip, glu_w, glu_b):
    t = proj.shape[0]
    w = dm.s5w
    tm = min(512, t)
    row = lambda c: pl.BlockSpec((tm, w), lambda i: (i, c))
    vec = pl.BlockSpec((1, w), lambda i: (0, 0))
    return pl.pallas_call(
        _s5_out_kernel,
        grid=(t // tm,),
        in_specs=[row(dm.o_s5 // w), row(0), row(0), vec, pl.BlockSpec((w, w), lambda i: (0, 0)), vec],
        out_specs=row(0),
        out_shape=jax.ShapeDtypeStruct((t, w), BF16),
        compiler_params=_cparams("parallel"),
        name="s5_out",
    )(proj, yf, yb, d_skip.reshape(1, w), glu_w.astype(BF16), glu_b.reshape(1, w))


def _gla_kernel(qf_ref, kf_ref, vf_ref, sf_ref, qb_ref, kb_ref, vb_ref, sb_ref, gu_ref, bias_ref,
                of_ref, ob_ref, st_ref, *, nc, npairs, tb, sq):
    i = pl.program_id(0)
    n = pl.num_programs(0)
    row = lax.broadcasted_iota(jnp.int32, (CHUNK, CHUNK), 0)
    col = lax.broadcasted_iota(jnp.int32, (CHUNK, CHUNK), 1)
    lane = lax.broadcasted_iota(jnp.int32, (CHUNK, LANES), 1)
    first_half = lane < GLA_DK
    scale = GLA_DK ** -0.5
    dirs = ((qf_ref, kf_ref, vf_ref, sf_ref, of_ref), (qb_ref, kb_ref, vb_ref, sb_ref, ob_ref))
    for d, (q_ref, k_ref, v_ref, s_ref, o_ref) in enumerate(dirs):
        rev = d == 1
        ti = (n - 1 - i) if rev else i
        _, first, last = _tile_seq(ti, tb, sq)

        @pl.when(last if rev else first)
        def _():
            st_ref[d] = jnp.zeros(st_ref.shape[1:], F32)

        tri = jnp.where((col >= row) if rev else (col <= row), 1.0, 0.0)
        keep = (col > row) if rev else (col <= row)
        for cc in range(nc):
            c = (nc - 1 - cc) if rev else cc
            rows = slice(CHUNK * c, CHUNK * (c + 1))
            z = _dot(s_ref[rows, :], gu_ref[d]) + bias_ref[d]
            log_a = _log_sigmoid(z) * (1.0 / GLA_GATE_NORM)
            b = _dot(tri, log_a, precision=HIGHEST)
            b_last = b[0:1, :] if rev else b[CHUNK - 1:CHUNK, :]
            q = q_ref[rows, :] * scale
            k = k_ref[rows, :]
            q_dec = q * jnp.exp(b)
            k_inv = k * jnp.exp(-b)
            k_dec = k * jnp.exp(b_last - b)
            dec = jnp.exp(b_last)
            for p in range(npairs):
                sl = slice(LANES * p, LANES * (p + 1))
                qp, kip, kdp = q_dec[:, sl], k_inv[:, sl], k_dec[:, sl]
                st = st_ref[d, p]
                new = st * dec[:, sl]
                for half in range(2):
                    h = 2 * p + half
                    msk = first_half if half == 0 else jnp.logical_not(first_half)
                    qm = jnp.where(msk, qp, 0.0)
                    attn = jnp.where(keep, _dot_nt(qm, kip), 0.0)
                    vh = v_ref[rows, GLA_DV * h:GLA_DV * (h + 1)]
                    o_ref[rows, GLA_DV * h:GLA_DV * (h + 1)] = _dot(attn, vh) + _dot_nt(qm, st)
                    new = new + _dot_tn(vh, jnp.where(msk, kdp, 0.0))
                st_ref[d, p] = new


def _gla_call(proj, dm, sq, gate_up, gate_bias):
    t = proj.shape[0]
    nc = 4
    tb = min(nc * CHUNK, sq.l1, sq.l2)
    nc = tb // CHUNK
    n = t // tb
    npairs = dm.gla_h // 2
    qk_w = dm.gla_h * GLA_DK
    gu = jnp.zeros((N_DIR, LANES, qk_w), F32)
    for d in range(N_DIR):
        gu = gu.at[d, d * GLA_LOW_RANK:(d + 1) * GLA_LOW_RANK, :].set(gate_up[d])
    bias = gate_bias.reshape(N_DIR, 1, qk_w)

    def specs(tile):
        return [pl.BlockSpec((tb, qk_w), lambda i: (tile(i), dm.o_gq // qk_w)),
                pl.BlockSpec((tb, qk_w), lambda i: (tile(i), dm.o_gk // qk_w)),
                pl.BlockSpec((tb, dm.gla_w), lambda i: (tile(i), dm.o_gv // dm.gla_w)),
                pl.BlockSpec((tb, LANES), lambda i: (tile(i), dm.o_small // LANES))]

    fwd = lambda i: i
    bwd = lambda i: n - 1 - i
    full = lambda a: pl.BlockSpec(a.shape, lambda i: (0,) * a.ndim)
    return pl.pallas_call(
        functools.partial(_gla_kernel, nc=nc, npairs=npairs, tb=tb, sq=sq),
        grid=(n,),
        in_specs=specs(fwd) + specs(bwd) + [full(gu), full(bias)],
        out_specs=[pl.BlockSpec((tb, dm.gla_w), lambda i: (i, 0)),
                   pl.BlockSpec((tb, dm.gla_w), lambda i: (n - 1 - i, 0))],
        out_shape=[jax.ShapeDtypeStruct((t, dm.gla_w), F32)] * 2,
        scratch_shapes=[pltpu.VMEM((N_DIR, npairs, GLA_DV, LANES), F32)],
        compiler_params=_cparams("arbitrary"),
        name="gla_chunks",
    )(*([proj] * 8), gu, bias)


def _gla_out_kernel(of_ref, ob_ref, g_ref, gain_ref, o_ref, *, heads):
    for h in range(heads):
        sl = slice(GLA_DV * h, GLA_DV * (h + 1))
        o = of_ref[:, sl] + ob_ref[:, sl]
        r = lax.rsqrt(jnp.mean(o * o, axis=-1, keepdims=True) + NORM_EPS)
        o_ref[:, sl] = (o * r * gain_ref[...] * _silu(g_ref[:, sl])).astype(BF16)


def _gla_out_call(proj, of, ob, dm, gain):
    t = proj.shape[0]
    w = dm.gla_w
    tm = min(512, t)
    row = lambda c: pl.BlockSpec((tm, w), lambda i: (i, c))
    return pl.pallas_call(
        functools.partial(_gla_out_kernel, heads=dm.gla_h),
        grid=(t // tm,),
        in_specs=[row(0), row(0), row(dm.o_gg // w), pl.BlockSpec((1, GLA_DV), lambda i: (0, 0))],
        out_specs=row(0),
        out_shape=jax.ShapeDtypeStruct((t, w), BF16),
        compiler_params=_cparams("parallel"),
        name="gla_out",
    )(of, ob, proj, gain.reshape(1, GLA_DV))


GDN_BETA_LANE = N_DIR * GLA_LOW_RANK
GDN_HEADS_PER_STEP = 2


def _unit_triangular_solve(a, rhs, upper):
    nblk = CHUNK // SUBLANES
    a_blk = [a[SUBLANES * i:SUBLANES * (i + 1), :] for i in range(nblk)]
    x_blk = [rhs[SUBLANES * i:SUBLANES * (i + 1), :] for i in range(nblk)]
    order = range(CHUNK - 1, 0, -1) if upper else range(CHUNK - 1)
    for j in order:
        bj, rj = divmod(j, SUBLANES)
        x_row = x_blk[bj][rj:rj + 1, :]
        for i in (range(bj + 1) if upper else range(bj, nblk)):
            x_blk[i] = x_blk[i] - a_blk[i][:, j:j + 1] * x_row
    return jnp.concatenate(x_blk, axis=0)


def _gdn_wy_kernel(x_ref, prev_ref, next_ref, s_ref, cw_ref, alog_ref, dtb_ref,
                   u_ref, w_ref, qd_ref, kd_ref, qk_ref, gc_ref,
                   ext_ref, q_ref, k_ref, v_ref, beta_ref, gcx_ref, gct_ref, *, tm, heads, sq):
    i = pl.program_id(0)
    _, first, last = _tile_seq(i, tm, sq)
    halo = SUBLANES
    pad = GDN_CONV // 2
    w = heads * GDN_DK
    ext_ref[0:halo, :] = jnp.where(first, 0.0, prev_ref[...])
    ext_ref[halo:halo + tm, :] = x_ref[...]
    ext_ref[halo + tm:, :] = jnp.where(last, 0.0, next_ref[...])
    acc = cw_ref[0:1, :] * ext_ref[pl.ds(halo - pad, tm), :]
    for j in range(1, GDN_CONV):
        acc = acc + cw_ref[j:j + 1, :] * ext_ref[pl.ds(halo - pad + j, tm), :]
    y = _silu(acc)
    for h in range(heads):
        qh = y[:, GDN_DK * h:GDN_DK * (h + 1)]
        kh = y[:, w + GDN_DK * h:w + GDN_DK * (h + 1)]
        q_ref[h] = qh * lax.rsqrt(jnp.sum(qh * qh, axis=-1, keepdims=True) + NORM_EPS) * (GDN_DK ** -0.5)
        k_ref[h] = kh * lax.rsqrt(jnp.sum(kh * kh, axis=-1, keepdims=True) + NORM_EPS)
        v_ref[h] = y[:, 2 * w + GDN_DV * h:2 * w + GDN_DV * (h + 1)]
    sm = s_ref[...]
    beta = jax.nn.sigmoid(sm)
    g = -jnp.exp(alog_ref[...]) * _softplus(sm + dtb_ref[...])
    row = lax.broadcasted_iota(jnp.int32, (tm, tm), 0)
    col = lax.broadcasted_iota(jnp.int32, (tm, tm), 1)
    same = (row // CHUNK) == (col // CHUNK)
    a_lane = GDN_BETA_LANE + N_DIR * heads
    for d in range(N_DIR):
        tri = jnp.where(jnp.logical_and(same, (col >= row) if d == 1 else (col <= row)), 1.0, 0.0)
        gc = _dot(tri, g, precision=HIGHEST)
        gc_ref[d] = gc
        for h in range(heads):
            lb = GDN_BETA_LANE + d * heads + h
            la = a_lane + d * heads + h
            beta_ref[d, h] = jnp.broadcast_to(beta[:, lb:lb + 1], (tm, LANES))
            gcx_ref[d, h] = jnp.broadcast_to(gc[:, la:la + 1], (tm, LANES))
        for c in range(tm // CHUNK):
            gcm = gc[CHUNK * c:CHUNK * (c + 1), :]
            gct_ref[d, c] = jnp.concatenate([gcm, jnp.zeros_like(gcm)], axis=0).T

    r64 = lax.broadcasted_iota(jnp.int32, (CHUNK, CHUNK), 0)
    c64 = lax.broadcasted_iota(jnp.int32, (CHUNK, CHUNK), 1)
    zeros = jnp.zeros((CHUNK, LANES - CHUNK), F32)

    def body(hp, carry):
        loaded = []
        for j in range(GDN_HEADS_PER_STEP):
            h = hp * GDN_HEADS_PER_STEP + j
            for c in range(tm // CHUNK):
                rows = pl.ds(CHUNK * c, CHUNK)
                per_dir = [(beta_ref[d, h, rows, :], gcx_ref[d, h, rows, :],
                            gct_ref[d, c, pl.ds(a_lane + d * heads + h, 1), :]) for d in range(N_DIR)]
                loaded.append((h, rows, q_ref[h, rows, :], k_ref[h, rows, :], v_ref[h, rows, :], per_dir))
        stores = []
        for h, rows, q, k, v, per_dir in loaded:
            gram = _dot_nt(k, k)
            qkt = _dot_nt(q, k)
            for d, (bt, gc, gct) in enumerate(per_dir):
                rev = d == 1
                incl = (c64 >= r64) if rev else (c64 <= r64)
                strict = (c64 > r64) if rev else (c64 < r64)
                decay = jnp.exp(jnp.where(incl, gc[:, :CHUNK] - gct[:, :CHUNK], -jnp.inf))
                a = jnp.where(strict, bt[:, :CHUNK] * gram * decay, 0.0)
                kb = k * bt
                rhs = jnp.concatenate([v * bt, kb * jnp.exp(gc)], axis=1)
                uw = _unit_triangular_solve(a, rhs, rev)
                e = 0 if rev else CHUNK - 1
                qk = jnp.where(strict if rev else incl, qkt * decay, 0.0)
                stores.append((d, h, rows, uw[:, :GDN_DV], uw[:, GDN_DV:].astype(BF16),
                               (q * jnp.exp(gc)).astype(BF16),
                               (k * jnp.exp(gc[e:e + 1, :] - gc)).astype(BF16),
                               jnp.concatenate([qk, zeros], axis=1).astype(BF16)))
        for d, h, rows, u, w, qd, kd, qk in stores:
            u_ref[d, h, rows, :] = u
            w_ref[d, h, rows, :] = w
            qd_ref[d, h, rows, :] = qd
            kd_ref[d, h, rows, :] = kd
            qk_ref[d, h, rows, :] = qk
        return carry

    lax.fori_loop(0, heads // GDN_HEADS_PER_STEP, body, 0)


def _gdn_wy_call(proj, dm, sq, conv_w, a_log, dt_bias):
    t = proj.shape[0]
    heads = dm.gdn_h
    tm = min(2 * CHUNK, sq.l1, sq.l2)
    n = t // tm
    wq = 3 * dm.gdn_w
    hb = tm // SUBLANES
    a_lane = GDN_BETA_LANE + N_DIR * heads
    alog = jnp.zeros((1, LANES), F32).at[0, a_lane:a_lane + N_DIR * heads].set(a_log.reshape(-1))
    dtb = jnp.zeros((1, LANES), F32).at[0, a_lane:a_lane + N_DIR * heads].set(dt_bias.reshape(-1))
    cw = conv_w.reshape(GDN_CONV, wq)
    cb = dm.o_qkv // wq
    nhb = t // SUBLANES
    dhm = lambda dt: jax.ShapeDtypeStruct((N_DIR, heads, t, LANES), dt)
    dhm_spec = pl.BlockSpec((N_DIR, heads, tm, LANES), lambda i: (0, 0, i, 0))
    return pl.pallas_call(
        functools.partial(_gdn_wy_kernel, tm=tm, heads=heads, sq=sq),
        grid=(n,),
        in_specs=[pl.BlockSpec((tm, wq), lambda i: (i, cb)),
                  pl.BlockSpec((SUBLANES, wq), lambda i: (jnp.maximum(i * hb - 1, 0), cb)),
                  pl.BlockSpec((SUBLANES, wq), lambda i: (jnp.minimum((i + 1) * hb, nhb - 1), cb)),
                  pl.BlockSpec((tm, LANES), lambda i: (i, dm.o_small // LANES)),
                  pl.BlockSpec((GDN_CONV, wq), lambda i: (0, 0)),
                  pl.BlockSpec((1, LANES), lambda i: (0, 0)),
                  pl.BlockSpec((1, LANES), lambda i: (0, 0))],
        out_specs=[dhm_spec] * 5 + [pl.BlockSpec((N_DIR, tm, LANES), lambda i: (0, i, 0))],
        out_shape=[dhm(F32)] + [dhm(BF16)] * 4 + [jax.ShapeDtypeStruct((N_DIR, t, LANES), F32)],
        scratch_shapes=[pltpu.VMEM((tm + 2 * SUBLANES, wq), F32)]
        + [pltpu.VMEM((heads, tm, LANES), F32)] * 3
        + [pltpu.VMEM((N_DIR, heads, tm, LANES), F32)] * 2
        + [pltpu.VMEM((N_DIR, tm // CHUNK, LANES, LANES), F32)],
        compiler_params=_cparams("parallel"),
        name="gdn_wy",
    )(proj, proj, proj, proj, cw, alog, dtb)


def _gdn_scan_kernel(uf_ref, wf_ref, qdf_ref, kdf_ref, qkf_ref, gcf_ref,
                     ub_ref, wb_ref, qdb_ref, kdb_ref, qkb_ref, gcb_ref,
                     of_ref, ob_ref, s_ref, *, nc, heads, tb, sq):
    i = pl.program_id(0)
    n = pl.num_programs(0)
    a_lane = GDN_BETA_LANE + N_DIR * heads
    dirs = ((uf_ref, wf_ref, qdf_ref, kdf_ref, qkf_ref, gcf_ref, of_ref),
            (ub_ref, wb_ref, qdb_ref, kdb_ref, qkb_ref, gcb_ref, ob_ref))
    for d in range(N_DIR):
        rev = d == 1
        ti = (n - 1 - i) if rev else i
        _, first, last = _tile_seq(ti, tb, sq)

        @pl.when(last if rev else first)
        def _():
            s_ref[d] = jnp.zeros(s_ref.shape[1:], F32)

    for cc in range(nc):
        for d, (u_ref, w_ref, qd_ref, kd_ref, qk_ref, gc_ref, o_ref) in enumerate(dirs):
            rev = d == 1
            c = (nc - 1 - cc) if rev else cc
            rows = slice(CHUNK * c, CHUNK * (c + 1))
            e = CHUNK * c + (0 if rev else CHUNK - 1)
            for h in range(heads):
                state = s_ref[d, h]
                wq = _dot(jnp.concatenate([w_ref[0, h, rows, :], qd_ref[0, h, rows, :]], axis=0),
                          state.astype(BF16))
                v_new = (u_ref[0, h, rows, :] - wq[:CHUNK]).astype(BF16)
                o_ref[h, rows, :] = wq[CHUNK:] + _dot(qk_ref[0, h, rows, :CHUNK], v_new)
                la = a_lane + d * heads + h
                dec = jnp.exp(gc_ref[0, e:e + 1, la:la + 1])
                s_ref[d, h] = state * dec + _dot_tn(kd_ref[0, h, rows, :], v_new)


def _gdn_scan_call(u, w, qd, kd, qk, gc, dm, sq):
    _, heads, t, _ = u.shape
    nc = 2
    tb = min(nc * CHUNK, sq.l1, sq.l2)
    nc = tb // CHUNK
    n = t // tb
    fwd = lambda i: i
    bwd = lambda i: n - 1 - i
    hm = lambda tile: pl.BlockSpec((heads, tb, LANES), lambda i: (0, tile(i), 0))
    dhm = lambda d, tile: pl.BlockSpec((1, heads, tb, LANES), lambda i: (d, 0, tile(i), 0))
    cm = lambda d, tile: pl.BlockSpec((1, tb, LANES), lambda i: (d, tile(i), 0))
    per_dir = lambda d, tile: [dhm(d, tile)] * 5 + [cm(d, tile)]
    return pl.pallas_call(
        functools.partial(_gdn_scan_kernel, nc=nc, heads=heads, tb=tb, sq=sq),
        grid=(n,),
        in_specs=per_dir(0, fwd) + per_dir(1, bwd),
        out_specs=[hm(fwd), hm(bwd)],
        out_shape=[jax.ShapeDtypeStruct((heads, t, LANES), F32)] * 2,
        scratch_shapes=[pltpu.VMEM((N_DIR, heads, GDN_DK, GDN_DV), F32)],
        compiler_params=_cparams("arbitrary"),
        name="gdn_scan",
    )(u, w, qd, kd, qk, gc, u, w, qd, kd, qk, gc)


def _gdn_out_kernel(of_ref, ob_ref, z_ref, gain_ref, o_ref, *, heads):
    for h in range(heads):
        sl = slice(GDN_DV * h, GDN_DV * (h + 1))
        o = of_ref[h] + ob_ref[h]
        r = lax.rsqrt(jnp.mean(o * o, axis=-1, keepdims=True) + NORM_EPS)
        o_ref[:, sl] = (o * r * gain_ref[...] * _silu(z_ref[:, sl])).astype(BF16)


def _gdn_out_call(proj, of, ob, dm, gain):
    t = proj.shape[0]
    w = dm.gdn_w
    heads = dm.gdn_h
    tm = min(512, t)
    hm = pl.BlockSpec((heads, tm, LANES), lambda i: (0, i, 0))
    return pl.pallas_call(
        functools.partial(_gdn_out_kernel, heads=heads),
        grid=(t // tm,),
        in_specs=[hm, hm, pl.BlockSpec((tm, w), lambda i: (i, dm.o_z // w)),
                  pl.BlockSpec((1, GDN_DV), lambda i: (0, 0))],
        out_specs=pl.BlockSpec((tm, w), lambda i: (i, 0)),
        out_shape=jax.ShapeDtypeStruct((t, w), BF16),
        compiler_params=_cparams("parallel"),
        name="gdn_out",
    )(of, ob, proj, gain.reshape(1, GDN_DV))


def _ffn_act_kernel(g_ref, gp_ref, gn_ref, v_ref, vp_ref, vn_ref, cwg_ref, cwv_ref, o_ref, ext_ref,
                    *, tm, sq):
    i = pl.program_id(0)
    _, first, last = _tile_seq(i, tm, sq)
    halo = BF16_ROWS
    pad = FFN_CONV // 2

    def conv(x_ref, prev_ref, next_ref, cw_ref):
        ext_ref[0:halo, :] = jnp.where(first, 0.0, prev_ref[...].astype(F32))
        ext_ref[halo:halo + tm, :] = x_ref[...].astype(F32)
        ext_ref[halo + tm:, :] = jnp.where(last, 0.0, next_ref[...].astype(F32))
        acc = cw_ref[0:1, :] * ext_ref[pl.ds(halo - pad, tm), :]
        for j in range(1, FFN_CONV):
            acc = acc + cw_ref[j:j + 1, :] * ext_ref[pl.ds(halo - pad + j, tm), :]
        return acc

    gate = conv(g_ref, gp_ref, gn_ref, cwg_ref)
    val = conv(v_ref, vp_ref, vn_ref, cwv_ref)
    o_ref[...] = (_silu(gate) * val).astype(BF16)


def _ffn_act_call(h1, dm, sq, cw_gate, cw_val):
    t = h1.shape[0]
    f = dm.dffp
    tm = min(256, sq.l1, sq.l2)
    tc = min(MM_TILE, f)
    hb = tm // BF16_ROWS
    nhb = t // BF16_ROWS
    nv = f // tc
    main = lambda off: pl.BlockSpec((tm, tc), lambda i, j: (i, j + off))
    prev = lambda off: pl.BlockSpec((BF16_ROWS, tc), lambda i, j: (jnp.maximum(i * hb - 1, 0), j + off))
    nxt = lambda off: pl.BlockSpec((BF16_ROWS, tc), lambda i, j: (jnp.minimum((i + 1) * hb, nhb - 1), j + off))
    cws = pl.BlockSpec((FFN_CONV, tc), lambda i, j: (0, j))
    return pl.pallas_call(
        functools.partial(_ffn_act_kernel, tm=tm, sq=sq),
        grid=(t // tm, f // tc),
        in_specs=[main(0), prev(0), nxt(0), main(nv), prev(nv), nxt(nv), cws, cws],
        out_specs=pl.BlockSpec((tm, tc), lambda i, j: (i, j)),
        out_shape=jax.ShapeDtypeStruct((t, f), BF16),
        scratch_shapes=[pltpu.VMEM((tm + 2 * BF16_ROWS, tc), F32)],
        compiler_params=_cparams("parallel", "parallel"),
        name="ffn_conv_swiglu",
    )(h1, h1, h1, h1, h1, h1, cw_gate, cw_val)


def _pad_cols(w, n):
    return jnp.pad(w, ((0, 0), (0, n - w.shape[1])))


def _layout_w_in(w_in, dm):
    qk = dm.gla_h * GLA_DK
    widths = (dm.s5w, qk, qk, dm.gla_w, dm.gla_w, N_DIR * GLA_LOW_RANK, 3 * dm.gdn_w, dm.gdn_w,
              N_DIR * dm.gdn_h, N_DIR * dm.gdn_h)
    splits = np.cumsum(widths)[:-1]
    s5_u, gq, gk, gv, gg, glr, qkv, z, beta, a = jnp.split(w_in, splits, axis=1)
    small = _pad_cols(jnp.concatenate([glr, beta, a], axis=1), LANES)
    rows = w_in.shape[0]
    parts, cur = [], 0
    for off, blk in sorted(((dm.o_qkv, qkv), (dm.o_z, z), (dm.o_gv, gv), (dm.o_gg, gg), (dm.o_gq, gq),
                            (dm.o_gk, gk), (dm.o_small, small), (dm.o_s5, s5_u)), key=lambda p: p[0]):
        if off > cur:
            parts.append(jnp.zeros((rows, off - cur), BF16))
        parts.append(blk.astype(BF16))
        cur = off + blk.shape[1]
    if cur < dm.n_in_pad:
        parts.append(jnp.zeros((rows, dm.n_in_pad - cur), BF16))
    return jnp.concatenate(parts, axis=1)


def _layer(x, sq, dm, mod, norm_gains, w_in, s5, gla, gdn, w_out, ffn_up, ffn_conv, ffn_down,
           pending):
    d = dm.d
    mod3 = mod.reshape(mod.shape[0], 1, 6 * d)
    gains3 = norm_gains.reshape(4, 1, d)
    if pending is None:
        (h,) = _resnorm_call(x, sq, norm=(mod3, 1, 0, gains3, 0))
    else:
        x, h = _resnorm_call(x, sq, res=pending, norm=(mod3, 1, 0, gains3, 0))
    proj = _mm_call(h, _layout_w_in(w_in, dm), F32, "in_proj")

    wb, wc, coef = _s5_params(*s5[:7])
    yf, yb = _s5_scan_call(proj, dm, sq, wb, wc, coef)
    y_s5 = _s5_out_call(proj, yf, yb, dm, s5[7], s5[8], s5[9])

    of, ob = _gla_call(proj, dm, sq, gla[0], gla[1])
    y_gla = _gla_out_call(proj, of, ob, dm, gla[2])

    of, ob = _gdn_scan_call(*_gdn_wy_call(proj, dm, sq, gdn[0], gdn[1], gdn[2]), dm, sq)
    y_gdn = _gdn_out_call(proj, of, ob, dm, gdn[3])

    mixed = _mm_call(jnp.concatenate([y_s5, y_gla, y_gdn], axis=1), w_out.astype(BF16), F32, "out_proj")
    x, h = _resnorm_call(x, sq, res=(mixed, mod3, 2, gains3, 1), norm=(mod3, 4, 3, gains3, 2))
    f, fp = dm.dff, dm.dffp
    up_b = ffn_up.astype(BF16)
    up = jnp.concatenate([_pad_cols(up_b[:, :f], fp), _pad_cols(up_b[:, f:], fp)], axis=1)
    cw = ffn_conv.reshape(FFN_CONV, 2 * f)
    h1 = _mm_call(h, up, BF16, "ffn_up")
    act = _ffn_act_call(h1, dm, sq, _pad_cols(cw[:, :f], fp), _pad_cols(cw[:, f:], fp))
    down = jnp.pad(ffn_down.astype(BF16), ((0, fp - f), (0, 0)))
    f_out = _mm_call(act, down, F32, "ffn_down")
    return x, (f_out, mod3, 5, gains3, 3)


def kernel(x_prompt, x_sample, c_prompt, c_sample, ada_w, ada_b, norm_gains, w_in, s5_lambda_re, s5_lambda_im, s5_log_step, s5_b_re, s5_b_im, s5_c_re, s5_c_im, s5_d, s5_glu_w, s5_glu_b, gla_gate_up, gla_gate_bias, gla_norm, gdn_conv, gdn_a_log, gdn_dt_bias, gdn_norm, w_out, ffn_up, ffn_conv, ffn_down):
    b1, l1, d = x_prompt.shape
    b2, l2, _ = x_sample.shape
    sq = _Seq(b1, l1, b2, l2)
    dm = _make_dims(d)
    depth = ada_w.shape[0]
    x = jnp.concatenate([x_prompt.reshape(b1 * l1, d), x_sample.reshape(b2 * l2, d)], axis=0)
    c = jnp.concatenate([c_prompt, c_sample], axis=0)
    c = jnp.pad(c, ((0, _round_up(sq.nseq, 2 * SUBLANES) - sq.nseq), (0, 0)))
    mod = _mod_call(c, ada_w, ada_b)
    pending = None
    for l in range(depth):
        s5 = (s5_lambda_re[l], s5_lambda_im[l], s5_log_step[l], s5_b_re[l], s5_b_im[l], s5_c_re[l],
              s5_c_im[l], s5_d[l], s5_glu_w[l], s5_glu_b[l])
        gla = (gla_gate_up[l], gla_gate_bias[l], gla_norm[l])
        gdn = (gdn_conv[l], gdn_a_log[l], gdn_dt_bias[l], gdn_norm[l])
        x, pending = _layer(x, sq, dm, mod[l], norm_gains[l], w_in[l], s5, gla, gdn, w_out[l],
                            ffn_up[l], ffn_conv[l], ffn_down[l], pending)
    (x,) = _resnorm_call(x, sq, res=pending)
    return (x[:b1 * l1].reshape(b1, l1, d), x[b1 * l1:].reshape(b2, l2, d))
```

```python
import functools
import math
from typing import NamedTuple

import jax
import jax.numpy as jnp
import numpy as np
from jax import lax
from jax.experimental import pallas as pl
from jax.experimental.pallas import tpu as pltpu

F32 = jnp.float32
BF16 = jnp.bfloat16
HIGHEST = lax.Precision.HIGHEST

NORM_EPS = 1e-6
CHUNK = 64
LANES = 128
SUBLANES = 8
BF16_ROWS = 16
N_DIR = 2
S5_GROUP_CH = 16
S5_STATE = 64
S5_BLOCK_GROUPS = LANES // S5_GROUP_CH
S5_BLOCK_STATES = S5_BLOCK_GROUPS * S5_STATE
GLA_DV = 128
GLA_DK = 64
GLA_LOW_RANK = 16
GLA_GATE_NORM = 16.0
GDN_DK = 128
GDN_DV = 128
GDN_CONV = 5
FFN_CONV = 3
VMEM_LIMIT_BYTES = 56 * 1024 * 1024
MM_TILE = 1024


class _Seq(NamedTuple):
    b1: int
    l1: int
    b2: int
    l2: int

    @property
    def tokens(self):
        return self.b1 * self.l1 + self.b2 * self.l2

    @property
    def nseq(self):
        return self.b1 + self.b2


class _Dims(NamedTuple):
    d: int
    s5w: int
    gla_w: int
    gla_h: int
    gdn_w: int
    gdn_h: int
    dff: int
    dffp: int
    o_qkv: int
    o_z: int
    o_gv: int
    o_gg: int
    o_gq: int
    o_gk: int
    o_small: int
    o_s5: int
    n_in_pad: int


def _round_up(x, m):
    return (x + m - 1) // m * m


def _make_dims(d):
    s5w = d // 4
    gla_w = 3 * d // 8
    gla_h = gla_w // GLA_DV
    gdn_w = d - s5w - gla_w
    gdn_h = gdn_w // GDN_DV
    dff = 256 * ((8 * d // 3 + 255) // 256)
    dffp = _round_up(dff, MM_TILE)
    segs = [3 * gdn_w, gdn_w, gla_w, gla_w, gla_h * GLA_DK, gla_h * GLA_DK, LANES, s5w]
    offs = []
    cur = 0
    for w in segs:
        cur = _round_up(cur, w)
        offs.append(cur)
        cur += w
    return _Dims(d, s5w, gla_w, gla_h, gdn_w, gdn_h, dff, dffp, *offs, _round_up(cur, MM_TILE))


def _tile_seq(i, tm, sq):
    p1 = sq.l1 // tm
    p2 = sq.l2 // tm
    n1 = sq.b1 * p1
    in1 = i < n1
    j = jnp.maximum(i - n1, 0)
    seq = jnp.where(in1, i // p1, sq.b1 + j // p2)
    pos = jnp.where(in1, i % p1, j % p2)
    last = jnp.where(in1, p1 - 1, p2 - 1)
    return seq, pos == 0, pos == last


def _cparams(*sem):
    return pltpu.CompilerParams(dimension_semantics=sem, vmem_limit_bytes=VMEM_LIMIT_BYTES)


def _dot(a, b, **kw):
    return jnp.dot(a, b, preferred_element_type=F32, **kw)


def _dot_nt(a, b, **kw):
    return lax.dot_general(a, b, (((1,), (1,)), ((), ())), preferred_element_type=F32, **kw)


def _dot_tn(a, b, **kw):
    return lax.dot_general(a, b, (((0,), (0,)), ((), ())), preferred_element_type=F32, **kw)


def _softplus(x):
    return jnp.maximum(x, 0.0) + jnp.log1p(jnp.exp(-jnp.abs(x)))


def _log_sigmoid(x):
    return jnp.minimum(x, 0.0) - jnp.log1p(jnp.exp(-jnp.abs(x)))


def _silu(x):
    return x * jax.nn.sigmoid(x)


def _mod_kernel(c_ref, w_ref, b_ref, o_ref):
    a = _silu(c_ref[...]).astype(BF16)
    o_ref[0] = _dot(a, w_ref[0].astype(BF16)) + b_ref[0]


def _mod_call(c_rows, ada_w, ada_b):
    nl, d, n = ada_w.shape
    r = c_rows.shape[0]
    tn = min(512, n)
    return pl.pallas_call(
        _mod_kernel,
        grid=(nl, n // tn),
        in_specs=[pl.BlockSpec((r, d), lambda l, j: (0, 0)),
                  pl.BlockSpec((1, d, tn), lambda l, j: (l, 0, j)),
                  pl.BlockSpec((1, 1, tn), lambda l, j: (l, 0, j))],
        out_specs=pl.BlockSpec((1, r, tn), lambda l, j: (l, 0, j)),
        out_shape=jax.ShapeDtypeStruct((nl, r, n), F32),
        compiler_params=_cparams("parallel", "parallel"),
        name="adaln_mod",
    )(c_rows, ada_w, ada_b.reshape(nl, 1, n))


def _resnorm_kernel(*refs, has_res, has_norm):
    it = iter(refs)
    x_ref = next(it)
    if has_res:
        m_ref, gate_ref, ga_ref = next(it), next(it), next(it)
    if has_norm:
        gb_ref, scale_ref, shift_ref = next(it), next(it), next(it)
    if has_res:
        xo_ref = next(it)
    if has_norm:
        h_ref = next(it)
    x = x_ref[...]
    if has_res:
        m = m_ref[...]
        r = lax.rsqrt(jnp.mean(m * m, axis=-1, keepdims=True) + NORM_EPS)
        x = x + gate_ref[0] * (m * r * ga_ref[0])
        xo_ref[...] = x
    if has_norm:
        r = lax.rsqrt(jnp.mean(x * x, axis=-1, keepdims=True) + NORM_EPS)
        h = (x * r * gb_ref[0]) * (1.0 + scale_ref[0]) + shift_ref[0]
        h_ref[...] = h.astype(BF16)


def _resnorm_call(x, sq, *, res=None, norm=None):
    t, d = x.shape
    tm = min(256, sq.l1, sq.l2)
    row = pl.BlockSpec((tm, d), lambda i: (i, 0))

    def mod_spec(col):
        return pl.BlockSpec((1, 1, d), lambda i: (_tile_seq(i, tm, sq)[0], 0, col))

    def gain_spec(r):
        return pl.BlockSpec((1, 1, d), lambda i: (r, 0, 0))

    args, in_specs, out_shape, out_specs = [x], [row], [], []
    if res is not None:
        m, mod3, gate_col, gains3, gain_row = res
        args += [m, mod3, gains3]
        in_specs += [row, mod_spec(gate_col), gain_spec(gain_row)]
        out_shape.append(jax.ShapeDtypeStruct((t, d), F32))
        out_specs.append(row)
    if norm is not None:
        mod3, scale_col, shift_col, gains3, gain_row = norm
        args += [gains3, mod3, mod3]
        in_specs += [gain_spec(gain_row), mod_spec(scale_col), mod_spec(shift_col)]
        out_shape.append(jax.ShapeDtypeStruct((t, d), BF16))
        out_specs.append(row)
    return pl.pallas_call(
        functools.partial(_resnorm_kernel, has_res=res is not None, has_norm=norm is not None),
        grid=(t // tm,),
        in_specs=in_specs, out_specs=out_specs, out_shape=out_shape,
        compiler_params=_cparams("parallel"),
        name="resnorm",
    )(*args)


def _mm_kernel(x_ref, w_ref, o_ref):
    o_ref[...] = _dot(x_ref[...], w_ref[...]).astype(o_ref.dtype)


def _mm_call(x, w, out_dtype, name):
    m, k = x.shape
    n = w.shape[1]
    tile = MM_TILE if k <= 4 * MM_TILE else MM_TILE // 2
    tm = min(tile, m)
    tn = min(tile, n)
    return pl.pallas_call(
        _mm_kernel,
        grid=(m // tm, n // tn),
        in_specs=[pl.BlockSpec((tm, k), lambda i, j: (i, 0)),
                  pl.BlockSpec((k, tn), lambda i, j: (0, j))],
        out_specs=pl.BlockSpec((tm, tn), lambda i, j: (i, j)),
        out_shape=jax.ShapeDtypeStruct((m, n), out_dtype),
        compiler_params=_cparams("parallel", "arbitrary"),
        name=name,
    )(x, w)


def _s5_disc_kernel(lr_ref, li_ref, ls_ref, brt_ref, bit_ref, pw_re_ref, pw_im_ref, bb_re_ref, bb_im_ref):
    for d in range(N_DIR):
        lr = lr_ref[d]
        li = li_ref[d]
        delta = jnp.exp(ls_ref[d])
        for k in range(1, SUBLANES + 1):
            mag = jnp.exp(lr * delta * float(k))
            th = li * delta * float(k)
            pw_re_ref[d, k - 1:k, :] = mag * jnp.cos(th)
            pw_im_ref[d, k - 1:k, :] = mag * jnp.sin(th)
        mag = jnp.exp(lr * delta)
        nr = mag * jnp.cos(li * delta) - 1.0
        ni = mag * jnp.sin(li * delta)
        den = lr * lr + li * li
        fr = (nr * lr + ni * li) / den
        fi = (ni * lr - nr * li) / den
        br = brt_ref[d]
        bi = bit_ref[d]
        bb_re_ref[d] = fr * br - fi * bi
        bb_im_ref[d] = fr * bi + fi * br


def _s5_params(lam_re, lam_im, log_step, b_re, b_im, c_re, c_im):
    nd, g, p = lam_re.shape
    gp = g * p
    nb = g // S5_BLOCK_GROUPS
    ls = jnp.repeat(log_step, p, axis=-1).reshape(nd, 1, gp)
    brt = b_re.transpose(0, 3, 1, 2).reshape(nd, S5_GROUP_CH, gp)
    bit = b_im.transpose(0, 3, 1, 2).reshape(nd, S5_GROUP_CH, gp)
    full = lambda *shape: pl.BlockSpec(shape, lambda: (0,) * len(shape))
    pw_re, pw_im, bb_re, bb_im = pl.pallas_call(
        _s5_disc_kernel,
        in_specs=[full(nd, 1, gp), full(nd, 1, gp), full(nd, 1, gp),
                  full(nd, S5_GROUP_CH, gp), full(nd, S5_GROUP_CH, gp)],
        out_specs=[full(nd, SUBLANES, gp), full(nd, SUBLANES, gp),
                   full(nd, S5_GROUP_CH, gp), full(nd, S5_GROUP_CH, gp)],
        out_shape=[jax.ShapeDtypeStruct((nd, SUBLANES, gp), F32)] * 2
        + [jax.ShapeDtypeStruct((nd, S5_GROUP_CH, gp), F32)] * 2,
        name="s5_discretize",
    )(lam_re.reshape(nd, 1, gp), lam_im.reshape(nd, 1, gp), ls, brt, bit)

    eye = jnp.eye(S5_BLOCK_GROUPS, dtype=F32)

    def in_proj(bb):
        bb = bb.reshape(nd, S5_GROUP_CH, nb, S5_BLOCK_GROUPS, p)
        w = jnp.einsum('dhjgp,gk->djghkp', bb, eye)
        return w.reshape(nd, nb, LANES, S5_BLOCK_STATES)

    def out_proj(c):
        c = c.reshape(nd, nb, S5_BLOCK_GROUPS, S5_GROUP_CH, p)
        w = jnp.einsum('djghp,gk->djgpkh', c, eye)
        return w.reshape(nd, nb, S5_BLOCK_STATES, LANES)

    wb = jnp.concatenate([in_proj(bb_re), in_proj(bb_im)], axis=-1).astype(BF16)
    wc = jnp.concatenate([out_proj(c_re), -out_proj(c_im)], axis=-2).astype(BF16)

    rows = jnp.arange(SUBLANES)
    coefs = []
    for d in range(nd):
        rev = d == 1
        kinds = []
        for sh in (1, 2, 4):
            keep = (rows < SUBLANES - sh) if rev else (rows >= sh)
            for pw in (pw_re, pw_im):
                kinds.append(jnp.where(keep[:, None], pw[d, sh - 1][None, :], 0.0))
        order = (SUBLANES - 1 - rows) if rev else rows
        for pw in (pw_re, pw_im):
            kinds.append(pw[d][order])
        coefs.append(jnp.stack(kinds))
    coef = jnp.stack(coefs).reshape(nd, 8, SUBLANES, nb, S5_BLOCK_STATES).transpose(0, 3, 1, 2, 4)
    return wb, wc, coef


def _s5_scan_kernel(uf_ref, ub_ref, wb_ref, wc_ref, coef_ref, yf_ref, yb_ref, bu_ref, carry_ref,
                    *, tt, nb, sq):
    i = pl.program_id(0)
    n = pl.num_programs(0)
    ns = S5_BLOCK_STATES
    ngroups = tt // SUBLANES
    for d, (u_ref, y_ref) in enumerate(((uf_ref, yf_ref), (ub_ref, yb_ref))):
        rev = d == 1
        ti = (n - 1 - i) if rev else i
        _, first, last = _tile_seq(ti, tt, sq)

        @pl.when(last if rev else first)
        def _():
            carry_ref[d] = jnp.zeros(carry_ref.shape[1:], F32)

        for j in range(nb):
            ub = u_ref[:, LANES * j:LANES * (j + 1)].astype(BF16)
            bu_ref[d] = _dot(ub, wb_ref[d, j])

            def body(r, c, d=d, j=j, rev=rev):
                c_re, c_im = c
                rr = (ngroups - 1 - r) if rev else r
                row = pl.multiple_of(rr * SUBLANES, SUBLANES)
                x = bu_ref[d, pl.ds(row, SUBLANES), :]
                xr, xi = x[:, :ns], x[:, ns:]
                for lvl, sh in enumerate((1, 2, 4)):
                    ar = coef_ref[d, j, 2 * lvl]
                    ai = coef_ref[d, j, 2 * lvl + 1]
                    s = (SUBLANES - sh) if rev else sh
                    yr = pltpu.roll(xr, s, 0)
                    yi = pltpu.roll(xi, s, 0)
                    xr, xi = xr + ar * yr - ai * yi, xi + ar * yi + ai * yr
                pr = coef_ref[d, j, 6]
                pi = coef_ref[d, j, 7]
                sr = xr + pr * c_re - pi * c_im
                si = xi + pr * c_im + pi * c_re
                bu_ref[d, pl.ds(row, SUBLANES), :] = jnp.concatenate([sr, si], axis=1)
                e = 0 if rev else SUBLANES - 1
                return (jnp.broadcast_to(sr[e:e + 1, :], sr.shape),
                        jnp.broadcast_to(si[e:e + 1, :], si.shape))

            c0 = carry_ref[d, j]
            c_re, c_im = lax.fori_loop(0, ngroups, body, (c0[:, :ns], c0[:, ns:]), unroll=True)
            carry_ref[d, j] = jnp.concatenate([c_re, c_im], axis=1)
            y_ref[:, LANES * j:LANES * (j + 1)] = _dot(bu_ref[d].astype(BF16), wc_ref[d, j])


def _s5_scan_call(proj, dm, sq, wb, wc, coef):
    t = proj.shape[0]
    tt = min(256, sq.l1, sq.l2)
    n = t // tt
    nb = dm.s5w // LANES
    cb = dm.o_s5 // dm.s5w
    full = lambda a: pl.BlockSpec(a.shape, lambda i: (0,) * a.ndim)
    return pl.pallas_call(
        functools.partial(_s5_scan_kernel, tt=tt, nb=nb, sq=sq),
        grid=(n,),
        in_specs=[pl.BlockSpec((tt, dm.s5w), lambda i: (i, cb)),
                  pl.BlockSpec((tt, dm.s5w), lambda i: (n - 1 - i, cb)),
                  full(wb), full(wc), full(coef)],
        out_specs=[pl.BlockSpec((tt, dm.s5w), lambda i: (i, 0)),
                   pl.BlockSpec((tt, dm.s5w), lambda i: (n - 1 - i, 0))],
        out_shape=[jax.ShapeDtypeStruct((t, dm.s5w), F32)] * 2,
        scratch_shapes=[pltpu.VMEM((N_DIR, tt, 2 * S5_BLOCK_STATES), F32),
                        pltpu.VMEM((N_DIR, nb, SUBLANES, 2 * S5_BLOCK_STATES), F32)],
        compiler_params=_cparams("arbitrary"),
        name="s5_scan",
    )(proj, proj, wb, wc, coef)


def _s5_out_kernel(u_ref, yf_ref, yb_ref, d_ref, w_ref, b_ref, o_ref):
    y = u_ref[...] * d_ref[...] + yf_ref[...] + yb_ref[...]
    y = jax.nn.gelu(y)
    gate = jax.nn.sigmoid(_dot(y.astype(BF16), w_ref[...]) + b_ref[...])
    o_ref[...] = (y * gate).astype(BF16)


def _s5_out_call(proj, yf, yb, dm, d_skip, glu_w, glu_b):
    t = proj.shape[0]
    w = dm.s5w
    tm = min(512, t)
    row = lambda c: pl.BlockSpec((tm, w), lambda i: (i, c))
    vec = pl.BlockSpec((1, w), lambda i: (0, 0))
    return pl.pallas_call(
        _s5_out_kernel,
        grid=(t // tm,),
        in_specs=[row(dm.o_s5 // w), row(0), row(0), vec, pl.BlockSpec((w, w), lambda i: (0, 0)), vec],
        out_specs=row(0),
        out_shape=jax.ShapeDtypeStruct((t, w), BF16),
        compiler_params=_cparams("parallel"),
        name="s5_out",
    )(proj, yf, yb, d_skip.reshape(1, w), glu_w.astype(BF16), glu_b.reshape(1, w))


def _gla_kernel(qf_ref, kf_ref, vf_ref, sf_ref, qb_ref, kb_ref, vb_ref, sb_ref, gu_ref, bias_ref,
                of_ref, ob_ref, st_ref, *, nc, npairs, tb, sq):
    i = pl.program_id(0)
    n = pl.num_programs(0)
    row = lax.broadcasted_iota(jnp.int32, (CHUNK, CHUNK), 0)
    col = lax.broadcasted_iota(jnp.int32, (CHUNK, CHUNK), 1)
    lane = lax.broadcasted_iota(jnp.int32, (CHUNK, LANES), 1)
    first_half = lane < GLA_DK
    row2 = lax.broadcasted_iota(jnp.int32, (2 * CHUNK, LANES), 0)
    col2 = lax.broadcasted_iota(jnp.int32, (2 * CHUNK, LANES), 1)
    scale = GLA_DK ** -0.5
    dirs = ((qf_ref, kf_ref, vf_ref, sf_ref, of_ref), (qb_ref, kb_ref, vb_ref, sb_ref, ob_ref))
    for d, (q_ref, k_ref, v_ref, s_ref, o_ref) in enumerate(dirs):
        rev = d == 1
        ti = (n - 1 - i) if rev else i
        _, first, last = _tile_seq(ti, tb, sq)

        @pl.when(last if rev else first)
        def _():
            st_ref[d] = jnp.zeros(st_ref.shape[1:], F32)

        tri = jnp.where((col >= row) if rev else (col <= row), 1.0, 0.0)
        keep2 = jnp.logical_and((row2 // CHUNK) == (col2 // CHUNK),
                                (col2 % CHUNK > row2 % CHUNK) if rev else (col2 % CHUNK <= row2 % CHUNK))
        for cc in range(nc):
            c = (nc - 1 - cc) if rev else cc
            rows = slice(CHUNK * c, CHUNK * (c + 1))
            z = _dot(s_ref[rows, :], gu_ref[d]) + bias_ref[d]
            log_a = _log_sigmoid(z) * (1.0 / GLA_GATE_NORM)
            b = _dot(tri, log_a, precision=HIGHEST)
            b_last = b[0:1, :] if rev else b[CHUNK - 1:CHUNK, :]
            q = q_ref[rows, :] * scale
            k = k_ref[rows, :]
            q_dec = q * jnp.exp(b)
            k_inv = k * jnp.exp(-b)
            k_dec = k * jnp.exp(b_last - b)
            dec = jnp.exp(b_last)
            for p in range(npairs):
                sl = slice(LANES * p, LANES * (p + 1))
                qp, kip, kdp = q_dec[:, sl], k_inv[:, sl], k_dec[:, sl]
                st = st_ref[d, p]
                q2 = jnp.concatenate([jnp.where(first_half, qp, 0.0), jnp.where(first_half, 0.0, qp)], axis=0)
                kd2 = jnp.concatenate([jnp.where(first_half, kdp, 0.0), jnp.where(first_half, 0.0, kdp)], axis=0)
                attn = jnp.where(keep2, _dot_nt(q2, jnp.concatenate([kip, kip], axis=0)), 0.0)
                vt = v_ref[rows, GLA_DV * 2 * p:GLA_DV * 2 * (p + 1)]
                vt = jnp.concatenate([vt[:, :GLA_DV], vt[:, GLA_DV:]], axis=0).T
                o2 = _dot_nt(jnp.concatenate([q2, attn], axis=1), jnp.concatenate([st, vt], axis=1))
                o_ref[rows, GLA_DV * 2 * p:GLA_DV * (2 * p + 1)] = o2[:CHUNK]
                o_ref[rows, GLA_DV * (2 * p + 1):GLA_DV * (2 * p + 2)] = o2[CHUNK:]
                st_ref[d, p] = st * dec[:, sl] + _dot(vt, kd2)


def _gla_call(proj, dm, sq, gate_up, gate_bias):
    t = proj.shape[0]
    nc = 4
    tb = min(nc * CHUNK, sq.l1, sq.l2)
    nc = tb // CHUNK
    n = t // tb
    npairs = dm.gla_h // 2
    qk_w = dm.gla_h * GLA_DK
    gu = jnp.zeros((N_DIR, LANES, qk_w), F32)
    for d in range(N_DIR):
        gu = gu.at[d, d * GLA_LOW_RANK:(d + 1) * GLA_LOW_RANK, :].set(gate_up[d])
    bias = gate_bias.reshape(N_DIR, 1, qk_w)

    def specs(tile):
        return [pl.BlockSpec((tb, qk_w), lambda i: (tile(i), dm.o_gq // qk_w)),
                pl.BlockSpec((tb, qk_w), lambda i: (tile(i), dm.o_gk // qk_w)),
                pl.BlockSpec((tb, dm.gla_w), lambda i: (tile(i), dm.o_gv // dm.gla_w)),
                pl.BlockSpec((tb, LANES), lambda i: (tile(i), dm.o_small // LANES))]

    fwd = lambda i: i
    bwd = lambda i: n - 1 - i
    full = lambda a: pl.BlockSpec(a.shape, lambda i: (0,) * a.ndim)
    return pl.pallas_call(
        functools.partial(_gla_kernel, nc=nc, npairs=npairs, tb=tb, sq=sq),
        grid=(n,),
        in_specs=specs(fwd) + specs(bwd) + [full(gu), full(bias)],
        out_specs=[pl.BlockSpec((tb, dm.gla_w), lambda i: (i, 0)),
                   pl.BlockSpec((tb, dm.gla_w), lambda i: (n - 1 - i, 0))],
        out_shape=[jax.ShapeDtypeStruct((t, dm.gla_w), F32)] * 2,
        scratch_shapes=[pltpu.VMEM((N_DIR, npairs, GLA_DV, LANES), F32)],
        compiler_params=_cparams("arbitrary"),
        name="gla_chunks",
    )(*([proj] * 8), gu, bias)


def _gla_out_kernel(of_ref, ob_ref, g_ref, gain_ref, o_ref, *, heads):
    for h in range(heads):
        sl = slice(GLA_DV * h, GLA_DV * (h + 1))
        o = of_ref[:, sl] + ob_ref[:, sl]
        r = lax.rsqrt(jnp.mean(o * o, axis=-1, keepdims=True) + NORM_EPS)
        o_ref[:, sl] = (o * r * gain_ref[...] * _silu(g_ref[:, sl])).astype(BF16)


def _gla_out_call(proj, of, ob, dm, gain):
    t = proj.shape[0]
    w = dm.gla_w
    tm = min(512, t)
    row = lambda c: pl.BlockSpec((tm, w), lambda i: (i, c))
    return pl.pallas_call(
        functools.partial(_gla_out_kernel, heads=dm.gla_h),
        grid=(t // tm,),
        in_specs=[row(0), row(0), row(dm.o_gg // w), pl.BlockSpec((1, GLA_DV), lambda i: (0, 0))],
        out_specs=row(0),
        out_shape=jax.ShapeDtypeStruct((t, w), BF16),
        compiler_params=_cparams("parallel"),
        name="gla_out",
    )(of, ob, proj, gain.reshape(1, GLA_DV))


GDN_BETA_LANE = N_DIR * GLA_LOW_RANK
GDN_HEADS_PER_STEP = 3


def _paired_unit_triangular_solve(a, xs, upper, lo):
    nblk = CHUNK // SUBLANES
    a_blk = [a[SUBLANES * i:SUBLANES * (i + 1), :] for i in range(nblk)]
    x_blk = [[x[SUBLANES * i:SUBLANES * (i + 1), :] for i in range(nblk)] for x in xs]
    order = range(CHUNK - 1, 0, -1) if upper else range(CHUNK - 1)
    for j in order:
        bj, rj = divmod(j, SUBLANES)
        idx = jnp.where(lo, j, CHUNK + j)
        rows = [xb[bj][rj:rj + 1, :] for xb in x_blk]
        for i in (range(bj + 1) if upper else range(bj, nblk)):
            col = jnp.take_along_axis(a_blk[i], idx, axis=1)
            for xb, x_row in zip(x_blk, rows):
                xb[i] = xb[i] - col * x_row
    return [jnp.concatenate(xb, axis=0) for xb in x_blk]


def _gdn_wy_kernel(x_ref, prev_ref, next_ref, s_ref, cw_ref, alog_ref, dtb_ref,
                   u_ref, w_ref, qd_ref, kd_ref, qk_ref, gc_ref,
                   ext_ref, q_ref, k_ref, v_ref, beta_ref, gcx_ref, gct_ref, *, tm, heads, sq):
    i = pl.program_id(0)
    _, first, last = _tile_seq(i, tm, sq)
    halo = SUBLANES
    pad = GDN_CONV // 2
    w = heads * GDN_DK
    ext_ref[0:halo, :] = jnp.where(first, 0.0, prev_ref[...])
    ext_ref[halo:halo + tm, :] = x_ref[...]
    ext_ref[halo + tm:, :] = jnp.where(last, 0.0, next_ref[...])
    acc = cw_ref[0:1, :] * ext_ref[pl.ds(halo - pad, tm), :]
    for j in range(1, GDN_CONV):
        acc = acc + cw_ref[j:j + 1, :] * ext_ref[pl.ds(halo - pad + j, tm), :]
    y = _silu(acc)
    for h in range(heads):
        qh = y[:, GDN_DK * h:GDN_DK * (h + 1)]
        kh = y[:, w + GDN_DK * h:w + GDN_DK * (h + 1)]
        q_ref[h] = qh * lax.rsqrt(jnp.sum(qh * qh, axis=-1, keepdims=True) + NORM_EPS) * (GDN_DK ** -0.5)
        k_ref[h] = kh * lax.rsqrt(jnp.sum(kh * kh, axis=-1, keepdims=True) + NORM_EPS)
        v_ref[h] = y[:, 2 * w + GDN_DV * h:2 * w + GDN_DV * (h + 1)]
    sm = s_ref[...]
    beta = jax.nn.sigmoid(sm)
    g = -jnp.exp(alog_ref[...]) * _softplus(sm + dtb_ref[...])
    row = lax.broadcasted_iota(jnp.int32, (tm, tm), 0)
    col = lax.broadcasted_iota(jnp.int32, (tm, tm), 1)
    same = (row // CHUNK) == (col // CHUNK)
    a_lane = GDN_BETA_LANE + N_DIR * heads
    for d in range(N_DIR):
        tri = jnp.where(jnp.logical_and(same, (col >= row) if d == 1 else (col <= row)), 1.0, 0.0)
        gc = _dot(tri, g, precision=HIGHEST)
        gc_ref[d] = gc
        for h in range(heads):
            lb = GDN_BETA_LANE + d * heads + h
            la = a_lane + d * heads + h
            beta_ref[d, h] = jnp.broadcast_to(beta[:, lb:lb + 1], (tm, LANES))
            gcx_ref[d, h] = jnp.broadcast_to(gc[:, la:la + 1], (tm, LANES))
        for c in range(tm // CHUNK):
            gcm = gc[CHUNK * c:CHUNK * (c + 1), :]
            zero = jnp.zeros_like(gcm)
            gct_ref[d, c] = jnp.concatenate([gcm, zero] if c == 0 else [zero, gcm], axis=0).T

    row = lax.broadcasted_iota(jnp.int32, (CHUNK, LANES), 0)
    lane = lax.broadcasted_iota(jnp.int32, (CHUNK, LANES), 1)
    lo = lane < CHUNK
    t_col = jnp.where(lo, lane, lane - CHUNK)
    lo_idx = lax.broadcasted_iota(jnp.int32, (SUBLANES, LANES), 1) < CHUNK

    def halves(x):
        return x[:CHUNK], x[CHUNK:]

    def pack(x):
        top, bot = halves(x)
        return [jnp.where(lo, top, pltpu.roll(bot, CHUNK, 1)),
                jnp.where(lo, pltpu.roll(top, CHUNK, 1), bot)]

    def unpack(p1, p2):
        return jnp.concatenate([jnp.where(lo, p1, pltpu.roll(p2, CHUNK, 1)),
                                jnp.where(lo, pltpu.roll(p1, CHUNK, 1), p2)], axis=0)

    def diag_blocks(x):
        top, bot = halves(x)
        return jnp.where(lo, top, bot)

    def body(hp, carry):
        loaded = []
        for j in range(GDN_HEADS_PER_STEP):
            h = hp * GDN_HEADS_PER_STEP + j
            per_dir = [(beta_ref[d, h], gcx_ref[d, h],
                        gct_ref[d, 0, pl.ds(a_lane + d * heads + h, 1), :]
                        + gct_ref[d, 1, pl.ds(a_lane + d * heads + h, 1), :]) for d in range(N_DIR)]
            loaded.append((h, q_ref[h], k_ref[h], v_ref[h], per_dir))
        stores = []
        for h, q, k, v, per_dir in loaded:
            gram = diag_blocks(_dot_nt(k, k))
            qkt = diag_blocks(_dot_nt(q, k))
            for d, (bt, gc, gc_row) in enumerate(per_dir):
                rev = d == 1
                incl = (t_col >= row) if rev else (t_col <= row)
                strict = (t_col > row) if rev else (t_col < row)
                decay = jnp.exp(jnp.where(incl, diag_blocks(gc) - gc_row, -jnp.inf))
                a = jnp.where(strict, diag_blocks(bt) * gram * decay, 0.0)
                kb = k * bt
                xs = _paired_unit_triangular_solve(a, pack(v * bt) + pack(kb * jnp.exp(gc)), rev, lo_idx)
                e = 0 if rev else CHUNK - 1
                gc_last = jnp.concatenate(
                    [jnp.broadcast_to(gc[e:e + 1, :], (CHUNK, LANES)),
                     jnp.broadcast_to(gc[CHUNK + e:CHUNK + e + 1, :], (CHUNK, LANES))], axis=0)
                qk = jnp.where(strict if rev else incl, qkt * decay, 0.0)
                stores.append((d, h, unpack(xs[0], xs[1]), unpack(xs[2], xs[3]).astype(BF16),
                               (q * jnp.exp(gc)).astype(BF16), (k * jnp.exp(gc_last - gc)).T.astype(BF16),
                               qk.astype(BF16)))
        for d, h, u, w, qd, kd, qk in stores:
            u_ref[d, h] = u
            w_ref[d, h] = w
            qd_ref[d, h] = qd
            kd_ref[d, h] = kd
            qk_ref[d, h] = qk
        return carry

    lax.fori_loop(0, heads // GDN_HEADS_PER_STEP, body, 0)


def _gdn_wy_call(proj, dm, sq, conv_w, a_log, dt_bias):
    t = proj.shape[0]
    heads = dm.gdn_h
    tm = 2 * CHUNK
    assert sq.l1 % tm == 0 and sq.l2 % tm == 0 and heads % GDN_HEADS_PER_STEP == 0
    n = t // tm
    wq = 3 * dm.gdn_w
    hb = tm // SUBLANES
    a_lane = GDN_BETA_LANE + N_DIR * heads
    alog = jnp.zeros((1, LANES), F32).at[0, a_lane:a_lane + N_DIR * heads].set(a_log.reshape(-1))
    dtb = jnp.zeros((1, LANES), F32).at[0, a_lane:a_lane + N_DIR * heads].set(dt_bias.reshape(-1))
    cw = conv_w.reshape(GDN_CONV, wq)
    cb = dm.o_qkv // wq
    nhb = t // SUBLANES
    dhm = lambda dt: jax.ShapeDtypeStruct((N_DIR, heads, t, LANES), dt)
    dhm_spec = pl.BlockSpec((N_DIR, heads, tm, LANES), lambda i: (0, 0, i, 0))
    return pl.pallas_call(
        functools.partial(_gdn_wy_kernel, tm=tm, heads=heads, sq=sq),
        grid=(n,),
        in_specs=[pl.BlockSpec((tm, wq), lambda i: (i, cb)),
                  pl.BlockSpec((SUBLANES, wq), lambda i: (jnp.maximum(i * hb - 1, 0), cb)),
                  pl.BlockSpec((SUBLANES, wq), lambda i: (jnp.minimum((i + 1) * hb, nhb - 1), cb)),
                  pl.BlockSpec((tm, LANES), lambda i: (i, dm.o_small // LANES)),
                  pl.BlockSpec((GDN_CONV, wq), lambda i: (0, 0)),
                  pl.BlockSpec((1, LANES), lambda i: (0, 0)),
                  pl.BlockSpec((1, LANES), lambda i: (0, 0))],
        out_specs=[dhm_spec] * 4
        + [pl.BlockSpec((N_DIR, heads, CHUNK, LANES), lambda i: (0, 0, i, 0)),
           pl.BlockSpec((N_DIR, tm, LANES), lambda i: (0, i, 0))],
        out_shape=[dhm(F32)] + [dhm(BF16)] * 3
        + [jax.ShapeDtypeStruct((N_DIR, heads, t // 2, LANES), BF16),
           jax.ShapeDtypeStruct((N_DIR, t, LANES), F32)],
        scratch_shapes=[pltpu.VMEM((tm + 2 * SUBLANES, wq), F32)]
        + [pltpu.VMEM((heads, tm, LANES), F32)] * 3
        + [pltpu.VMEM((N_DIR, heads, tm, LANES), F32)] * 2
        + [pltpu.VMEM((N_DIR, tm // CHUNK, LANES, LANES), F32)],
        compiler_params=_cparams("parallel"),
        name="gdn_wy",
    )(proj, proj, proj, proj, cw, alog, dtb)


def _gdn_scan_kernel(uf_ref, wf_ref, qdf_ref, kdf_ref, qkf_ref, gcf_ref,
                     ub_ref, wb_ref, qdb_ref, kdb_ref, qkb_ref, gcb_ref,
                     of_ref, ob_ref, s_ref, *, nc, heads, tb, sq):
    i = pl.program_id(0)
    n = pl.num_programs(0)
    a_lane = GDN_BETA_LANE + N_DIR * heads
    dirs = ((uf_ref, wf_ref, qdf_ref, kdf_ref, qkf_ref, gcf_ref, of_ref),
            (ub_ref, wb_ref, qdb_ref, kdb_ref, qkb_ref, gcb_ref, ob_ref))
    for d in range(N_DIR):
        rev = d == 1
        ti = (n - 1 - i) if rev else i
        _, first, last = _tile_seq(ti, tb, sq)

        @pl.when(last if rev else first)
        def _():
            s_ref[d] = jnp.zeros(s_ref.shape[1:], F32)

    lo = lax.broadcasted_iota(jnp.int32, (CHUNK + GDN_DK, LANES), 1) < CHUNK
    zeros = jnp.zeros((CHUNK, GDN_DV), BF16)
    for cc in range(nc):
        for d, (u_ref, w_ref, qd_ref, kd_ref, qk_ref, gc_ref, o_ref) in enumerate(dirs):
            rev = d == 1
            c = (nc - 1 - cc) if rev else cc
            rows = slice(CHUNK * c, CHUNK * (c + 1))
            e = CHUNK * c + (0 if rev else CHUNK - 1)
            for h in range(heads):
                state = s_ref[d, h]
                wq = _dot(jnp.concatenate([w_ref[0, h, rows, :], qd_ref[0, h, rows, :]], axis=0),
                          state.astype(BF16))
                v_new = (u_ref[0, h, rows, :] - wq[:CHUNK]).astype(BF16)
                both = jnp.concatenate([qk_ref[0, h, CHUNK * (c // 2):CHUNK * (c // 2 + 1), :],
                                        kd_ref[0, h, 2 * CHUNK * (c // 2):2 * CHUNK * (c // 2 + 1), :]], axis=0)
                if c % 2 == 0:
                    res = _dot(jnp.where(lo, both, 0), jnp.concatenate([v_new, zeros], axis=0))
                else:
                    res = _dot(jnp.where(lo, 0, both), jnp.concatenate([zeros, v_new], axis=0))
                o_ref[h, rows, :] = wq[CHUNK:] + res[:CHUNK]
                la = a_lane + d * heads + h
                dec = jnp.exp(gc_ref[0, e:e + 1, la:la + 1])
                s_ref[d, h] = state * dec + res[CHUNK:]


def _gdn_scan_call(u, w, qd, kd, qk, gc, dm, sq):
    _, heads, t, _ = u.shape
    nc = 2
    tb = min(nc * CHUNK, sq.l1, sq.l2)
    nc = tb // CHUNK
    n = t // tb
    fwd = lambda i: i
    bwd = lambda i: n - 1 - i
    hm = lambda tile: pl.BlockSpec((heads, tb, LANES), lambda i: (0, tile(i), 0))
    dhm = lambda d, tile: pl.BlockSpec((1, heads, tb, LANES), lambda i: (d, 0, tile(i), 0))
    cm = lambda d, tile: pl.BlockSpec((1, tb, LANES), lambda i: (d, tile(i), 0))
    qkm = lambda d, tile: pl.BlockSpec((1, heads, tb // 2, LANES), lambda i: (d, 0, tile(i), 0))
    per_dir = lambda d, tile: [dhm(d, tile)] * 4 + [qkm(d, tile), cm(d, tile)]
    return pl.pallas_call(
        functools.partial(_gdn_scan_kernel, nc=nc, heads=heads, tb=tb, sq=sq),
        grid=(n,),
        in_specs=per_dir(0, fwd) + per_dir(1, bwd),
        out_specs=[hm(fwd), hm(bwd)],
        out_shape=[jax.ShapeDtypeStruct((heads, t, LANES), F32)] * 2,
        scratch_shapes=[pltpu.VMEM((N_DIR, heads, GDN_DK, GDN_DV), F32)],
        compiler_params=_cparams("arbitrary"),
        name="gdn_scan",
    )(u, w, qd, kd, qk, gc, u, w, qd, kd, qk, gc)


def _gdn_out_kernel(of_ref, ob_ref, z_ref, gain_ref, o_ref, *, heads):
    for h in range(heads):
        sl = slice(GDN_DV * h, GDN_DV * (h + 1))
        o = of_ref[h] + ob_ref[h]
        r = lax.rsqrt(jnp.mean(o * o, axis=-1, keepdims=True) + NORM_EPS)
        o_ref[:, sl] = (o * r * gain_ref[...] * _silu(z_ref[:, sl])).astype(BF16)


def _gdn_out_call(proj, of, ob, dm, gain):
    t = proj.shape[0]
    w = dm.gdn_w
    heads = dm.gdn_h
    tm = min(512, t)
    hm = pl.BlockSpec((heads, tm, LANES), lambda i: (0, i, 0))
    return pl.pallas_call(
        functools.partial(_gdn_out_kernel, heads=heads),
        grid=(t // tm,),
        in_specs=[hm, hm, pl.BlockSpec((tm, w), lambda i: (i, dm.o_z // w)),
                  pl.BlockSpec((1, GDN_DV), lambda i: (0, 0))],
        out_specs=pl.BlockSpec((tm, w), lambda i: (i, 0)),
        out_shape=jax.ShapeDtypeStruct((t, w), BF16),
        compiler_params=_cparams("parallel"),
        name="gdn_out",
    )(of, ob, proj, gain.reshape(1, GDN_DV))


def _ffn_act_kernel(g_ref, gp_ref, gn_ref, v_ref, vp_ref, vn_ref, cwg_ref, cwv_ref, o_ref, ext_ref,
                    *, tm, sq):
    i = pl.program_id(0)
    _, first, last = _tile_seq(i, tm, sq)
    halo = BF16_ROWS
    pad = FFN_CONV // 2

    def conv(x_ref, prev_ref, next_ref, cw_ref):
        ext_ref[0:halo, :] = jnp.where(first, 0.0, prev_ref[...].astype(F32))
        ext_ref[halo:halo + tm, :] = x_ref[...].astype(F32)
        ext_ref[halo + tm:, :] = jnp.where(last, 0.0, next_ref[...].astype(F32))
        acc = cw_ref[0:1, :] * ext_ref[pl.ds(halo - pad, tm), :]
        for j in range(1, FFN_CONV):
            acc = acc + cw_ref[j:j + 1, :] * ext_ref[pl.ds(halo - pad + j, tm), :]
        return acc

    gate = conv(g_ref, gp_ref, gn_ref, cwg_ref)
    val = conv(v_ref, vp_ref, vn_ref, cwv_ref)
    o_ref[...] = (_silu(gate) * val).astype(BF16)


def _ffn_act_call(h1, dm, sq, cw_gate, cw_val):
    t = h1.shape[0]
    f = dm.dffp
    tm = min(256, sq.l1, sq.l2)
    tc = min(MM_TILE, f)
    hb = tm // BF16_ROWS
    nhb = t // BF16_ROWS
    nv = f // tc
    main = lambda off: pl.BlockSpec((tm, tc), lambda i, j: (i, j + off))
    prev = lambda off: pl.BlockSpec((BF16_ROWS, tc), lambda i, j: (jnp.maximum(i * hb - 1, 0), j + off))
    nxt = lambda off: pl.BlockSpec((BF16_ROWS, tc), lambda i, j: (jnp.minimum((i + 1) * hb, nhb - 1), j + off))
    cws = pl.BlockSpec((FFN_CONV, tc), lambda i, j: (0, j))
    return pl.pallas_call(
        functools.partial(_ffn_act_kernel, tm=tm, sq=sq),
        grid=(t // tm, f // tc),
        in_specs=[main(0), prev(0), nxt(0), main(nv), prev(nv), nxt(nv), cws, cws],
        out_specs=pl.BlockSpec((tm, tc), lambda i, j: (i, j)),
        out_shape=jax.ShapeDtypeStruct((t, f), BF16),
        scratch_shapes=[pltpu.VMEM((tm + 2 * BF16_ROWS, tc), F32)],
        compiler_params=_cparams("parallel", "parallel"),
        name="ffn_conv_swiglu",
    )(h1, h1, h1, h1, h1, h1, cw_gate, cw_val)


def _pad_cols(w, n):
    return jnp.pad(w, ((0, 0), (0, n - w.shape[1])))


def _layout_w_in(w_in, dm):
    qk = dm.gla_h * GLA_DK
    widths = (dm.s5w, qk, qk, dm.gla_w, dm.gla_w, N_DIR * GLA_LOW_RANK, 3 * dm.gdn_w, dm.gdn_w,
              N_DIR * dm.gdn_h, N_DIR * dm.gdn_h)
    splits = np.cumsum(widths)[:-1]
    s5_u, gq, gk, gv, gg, glr, qkv, z, beta, a = jnp.split(w_in, splits, axis=1)
    small = _pad_cols(jnp.concatenate([glr, beta, a], axis=1), LANES)
    rows = w_in.shape[0]
    parts, cur = [], 0
    for off, blk in sorted(((dm.o_qkv, qkv), (dm.o_z, z), (dm.o_gv, gv), (dm.o_gg, gg), (dm.o_gq, gq),
                            (dm.o_gk, gk), (dm.o_small, small), (dm.o_s5, s5_u)), key=lambda p: p[0]):
        if off > cur:
            parts.append(jnp.zeros((rows, off - cur), BF16))
        parts.append(blk.astype(BF16))
        cur = off + blk.shape[1]
    if cur < dm.n_in_pad:
        parts.append(jnp.zeros((rows, dm.n_in_pad - cur), BF16))
    return jnp.concatenate(parts, axis=1)


def _layer(x, sq, dm, mod, norm_gains, w_in, s5, gla, gdn, w_out, ffn_up, ffn_conv, ffn_down,
           pending):
    d = dm.d
    mod3 = mod.reshape(mod.shape[0], 1, 6 * d)
    gains3 = norm_gains.reshape(4, 1, d)
    if pending is None:
        (h,) = _resnorm_call(x, sq, norm=(mod3, 1, 0, gains3, 0))
    else:
        x, h = _resnorm_call(x, sq, res=pending, norm=(mod3, 1, 0, gains3, 0))
    proj = _mm_call(h, _layout_w_in(w_in, dm), F32, "in_proj")

    wb, wc, coef = _s5_params(*s5[:7])
    yf, yb = _s5_scan_call(proj, dm, sq, wb, wc, coef)
    y_s5 = _s5_out_call(proj, yf, yb, dm, s5[7], s5[8], s5[9])

    of, ob = _gla_call(proj, dm, sq, gla[0], gla[1])
    y_gla = _gla_out_call(proj, of, ob, dm, gla[2])

    of, ob = _gdn_scan_call(*_gdn_wy_call(proj, dm, sq, gdn[0], gdn[1], gdn[2]), dm, sq)
    y_gdn = _gdn_out_call(proj, of, ob, dm, gdn[3])

    mixed = _mm_call(jnp.concatenate([y_s5, y_gla, y_gdn], axis=1), w_out.astype(BF16), F32, "out_proj")
    x, h = _resnorm_call(x, sq, res=(mixed, mod3, 2, gains3, 1), norm=(mod3, 4, 3, gains3, 2))
    f, fp = dm.dff, dm.dffp
    up_b = ffn_up.astype(BF16)
    up = jnp.concatenate([_pad_cols(up_b[:, :f], fp), _pad_cols(up_b[:, f:], fp)], axis=1)
    cw = ffn_conv.reshape(FFN_CONV, 2 * f)
    h1 = _mm_call(h, up, BF16, "ffn_up")
    act = _ffn_act_call(h1, dm, sq, _pad_cols(cw[:, :f], fp), _pad_cols(cw[:, f:], fp))
    down = jnp.pad(ffn_down.astype(BF16), ((0, fp - f), (0, 0)))
    f_out = _mm_call(act, down, F32, "ffn_down")
    return x, (f_out, mod3, 5, gains3, 3)


def kernel(x_prompt, x_sample, c_prompt, c_sample, ada_w, ada_b, norm_gains, w_in, s5_lambda_re, s5_lambda_im, s5_log_step, s5_b_re, s5_b_im, s5_c_re, s5_c_im, s5_d, s5_glu_w, s5_glu_b, gla_gate_up, gla_gate_bias, gla_norm, gdn_conv, gdn_a_log, gdn_dt_bias, gdn_norm, w_out, ffn_up, ffn_conv, ffn_down):
    b1, l1, d = x_prompt.shape
    b2, l2, _ = x_sample.shape
    sq = _Seq(b1, l1, b2, l2)
    dm = _make_dims(d)
    depth = ada_w.shape[0]
    x = jnp.concatenate([x_prompt.reshape(b1 * l1, d), x_sample.reshape(b2 * l2, d)], axis=0)
    c = jnp.concatenate([c_prompt, c_sample], axis=0)
    c = jnp.pad(c, ((0, _round_up(sq.nseq, 2 * SUBLANES) - sq.nseq), (0, 0)))
    mod = _mod_call(c, ada_w, ada_b)
    pending = None
    for l in range(depth):
        s5 = (s5_lambda_re[l], s5_lambda_im[l], s5_log_step[l], s5_b_re[l], s5_b_im[l], s5_c_re[l],
              s5_c_im[l], s5_d[l], s5_glu_w[l], s5_glu_b[l])
        gla = (gla_gate_up[l], gla_gate_bias[l], gla_norm[l])
        gdn = (gdn_conv[l], gdn_a_log[l], gdn_dt_bias[l], gdn_norm[l])
        x, pending = _layer(x, sq, dm, mod[l], norm_gains[l], w_in[l], s5, gla, gdn, w_out[l],
                            ffn_up[l], ffn_conv[l], ffn_down[l], pending)
    (x,) = _resnorm_call(x, sq, res=pending)
    return (x[:b1 * l1].reshape(b1, l1, d), x[b1 * l1:].reshape(b2, l2, d))
```

```python
import functools
import math
from typing import NamedTuple

import jax
import jax.numpy as jnp
import numpy as np
from jax import lax
from jax.experimental import pallas as pl
from jax.experimental.pallas import tpu as pltpu

F32 = jnp.float32
BF16 = jnp.bfloat16
HIGHEST = lax.Precision.HIGHEST

NORM_EPS = 1e-6
CHUNK = 64
LANES = 128
SUBLANES = 8
BF16_ROWS = 16
N_DIR = 2
S5_GROUP_CH = 16
S5_STATE = 64
S5_BLOCK_GROUPS = LANES // S5_GROUP_CH
S5_BLOCK_STATES = S5_BLOCK_GROUPS * S5_STATE
S5_TILE = 256
S5_SEG = S5_TILE // SUBLANES
S5_POWERS = tuple(range(1, S5_SEG + 1)) + tuple(S5_SEG * s for s in range(2, SUBLANES + 1))
GLA_DV = 128
GLA_DK = 64
GLA_LOW_RANK = 16
GLA_GATE_NORM = 16.0
GDN_DK = 128
GDN_DV = 128
GDN_CONV = 5
FFN_CONV = 3
VMEM_LIMIT_BYTES = 56 * 1024 * 1024
MM_TILE = 1024


class _Seq(NamedTuple):
    b1: int
    l1: int
    b2: int
    l2: int

    @property
    def tokens(self):
        return self.b1 * self.l1 + self.b2 * self.l2

    @property
    def nseq(self):
        return self.b1 + self.b2


class _Dims(NamedTuple):
    d: int
    s5w: int
    gla_w: int
    gla_h: int
    gdn_w: int
    gdn_h: int
    dff: int
    dffp: int
    o_qkv: int
    o_z: int
    o_gv: int
    o_gg: int
    o_gq: int
    o_gk: int
    o_small: int
    o_s5: int
    n_in_pad: int


def _round_up(x, m):
    return (x + m - 1) // m * m


def _make_dims(d):
    s5w = d // 4
    gla_w = 3 * d // 8
    gla_h = gla_w // GLA_DV
    gdn_w = d - s5w - gla_w
    gdn_h = gdn_w // GDN_DV
    dff = 256 * ((8 * d // 3 + 255) // 256)
    dffp = _round_up(dff, MM_TILE)
    segs = [3 * gdn_w, gdn_w, gla_w, gla_w, gla_h * GLA_DK, gla_h * GLA_DK, LANES, s5w]
    offs = []
    cur = 0
    for w in segs:
        cur = _round_up(cur, w)
        offs.append(cur)
        cur += w
    return _Dims(d, s5w, gla_w, gla_h, gdn_w, gdn_h, dff, dffp, *offs, _round_up(cur, MM_TILE))


def _tile_seq(i, tm, sq):
    p1 = sq.l1 // tm
    p2 = sq.l2 // tm
    n1 = sq.b1 * p1
    in1 = i < n1
    j = jnp.maximum(i - n1, 0)
    seq = jnp.where(in1, i // p1, sq.b1 + j // p2)
    pos = jnp.where(in1, i % p1, j % p2)
    last = jnp.where(in1, p1 - 1, p2 - 1)
    return seq, pos == 0, pos == last


def _cparams(*sem):
    return pltpu.CompilerParams(dimension_semantics=sem, vmem_limit_bytes=VMEM_LIMIT_BYTES)


def _dot(a, b, **kw):
    return jnp.dot(a, b, preferred_element_type=F32, **kw)


def _dot_nt(a, b, **kw):
    return lax.dot_general(a, b, (((1,), (1,)), ((), ())), preferred_element_type=F32, **kw)


def _dot_tn(a, b, **kw):
    return lax.dot_general(a, b, (((0,), (0,)), ((), ())), preferred_element_type=F32, **kw)


def _softplus(x):
    return jnp.maximum(x, 0.0) + jnp.log1p(jnp.exp(-jnp.abs(x)))


def _log_sigmoid(x):
    return jnp.minimum(x, 0.0) - jnp.log1p(jnp.exp(-jnp.abs(x)))


def _silu(x):
    return x * jax.nn.sigmoid(x)


def _mod_kernel(c_ref, w_ref, b_ref, o_ref):
    a = _silu(c_ref[...]).astype(BF16)
    o_ref[0] = _dot(a, w_ref[0].astype(BF16)) + b_ref[0]


def _mod_call(c_rows, ada_w, ada_b):
    nl, d, n = ada_w.shape
    r = c_rows.shape[0]
    tn = min(512, n)
    return pl.pallas_call(
        _mod_kernel,
        grid=(nl, n // tn),
        in_specs=[pl.BlockSpec((r, d), lambda l, j: (0, 0)),
                  pl.BlockSpec((1, d, tn), lambda l, j: (l, 0, j)),
                  pl.BlockSpec((1, 1, tn), lambda l, j: (l, 0, j))],
        out_specs=pl.BlockSpec((1, r, tn), lambda l, j: (l, 0, j)),
        out_shape=jax.ShapeDtypeStruct((nl, r, n), F32),
        compiler_params=_cparams("parallel", "parallel"),
        name="adaln_mod",
    )(c_rows, ada_w, ada_b.reshape(nl, 1, n))


def _resnorm_kernel(*refs, has_res, has_norm):
    it = iter(refs)
    x_ref = next(it)
    if has_res:
        m_ref, gate_ref, ga_ref = next(it), next(it), next(it)
    if has_norm:
        gb_ref, scale_ref, shift_ref = next(it), next(it), next(it)
    if has_res:
        xo_ref = next(it)
    if has_norm:
        h_ref = next(it)
    x = x_ref[...]
    if has_res:
        m = m_ref[...]
        r = lax.rsqrt(jnp.mean(m * m, axis=-1, keepdims=True) + NORM_EPS)
        x = x + gate_ref[0] * (m * r * ga_ref[0])
        xo_ref[...] = x
    if has_norm:
        r = lax.rsqrt(jnp.mean(x * x, axis=-1, keepdims=True) + NORM_EPS)
        h = (x * r * gb_ref[0]) * (1.0 + scale_ref[0]) + shift_ref[0]
        h_ref[...] = h.astype(BF16)


def _resnorm_call(x, sq, *, res=None, norm=None):
    t, d = x.shape
    tm = min(256, sq.l1, sq.l2)
    row = pl.BlockSpec((tm, d), lambda i: (i, 0))

    def mod_spec(col):
        return pl.BlockSpec((1, 1, d), lambda i: (_tile_seq(i, tm, sq)[0], 0, col))

    def gain_spec(r):
        return pl.BlockSpec((1, 1, d), lambda i: (r, 0, 0))

    args, in_specs, out_shape, out_specs = [x], [row], [], []
    if res is not None:
        m, mod3, gate_col, gains3, gain_row = res
        args += [m, mod3, gains3]
        in_specs += [row, mod_spec(gate_col), gain_spec(gain_row)]
        out_shape.append(jax.ShapeDtypeStruct((t, d), F32))
        out_specs.append(row)
    if norm is not None:
        mod3, scale_col, shift_col, gains3, gain_row = norm
        args += [gains3, mod3, mod3]
        in_specs += [gain_spec(gain_row), mod_spec(scale_col), mod_spec(shift_col)]
        out_shape.append(jax.ShapeDtypeStruct((t, d), BF16))
        out_specs.append(row)
    return pl.pallas_call(
        functools.partial(_resnorm_kernel, has_res=res is not None, has_norm=norm is not None),
        grid=(t // tm,),
        in_specs=in_specs, out_specs=out_specs, out_shape=out_shape,
        compiler_params=_cparams("parallel"),
        name="resnorm",
    )(*args)


def _mm_kernel(x_ref, w_ref, o_ref):
    o_ref[...] = _dot(x_ref[...], w_ref[...]).astype(o_ref.dtype)


def _mm_call(x, w, out_dtype, name):
    m, k = x.shape
    n = w.shape[1]
    tile = MM_TILE if k <= 4 * MM_TILE else MM_TILE // 2
    tm = min(tile, m)
    tn = min(tile, n)
    return pl.pallas_call(
        _mm_kernel,
        grid=(m // tm, n // tn),
        in_specs=[pl.BlockSpec((tm, k), lambda i, j: (i, 0)),
                  pl.BlockSpec((k, tn), lambda i, j: (0, j))],
        out_specs=pl.BlockSpec((tm, tn), lambda i, j: (i, j)),
        out_shape=jax.ShapeDtypeStruct((m, n), out_dtype),
        compiler_params=_cparams("parallel", "arbitrary"),
        name=name,
    )(x, w)


def _s5_disc_kernel(lr_ref, li_ref, ls_ref, brt_ref, bit_ref, pw_re_ref, pw_im_ref, bb_re_ref, bb_im_ref):
    for d in range(N_DIR):
        lr = lr_ref[d]
        li = li_ref[d]
        delta = jnp.exp(ls_ref[d])
        for r, m in enumerate(S5_POWERS):
            mag = jnp.exp(lr * delta * float(m))
            th = li * delta * float(m)
            pw_re_ref[d, r:r + 1, :] = mag * jnp.cos(th)
            pw_im_ref[d, r:r + 1, :] = mag * jnp.sin(th)
        mag = jnp.exp(lr * delta)
        nr = mag * jnp.cos(li * delta) - 1.0
        ni = mag * jnp.sin(li * delta)
        den = lr * lr + li * li
        fr = (nr * lr + ni * li) / den
        fi = (ni * lr - nr * li) / den
        br = brt_ref[d]
        bi = bit_ref[d]
        bb_re_ref[d] = fr * br - fi * bi
        bb_im_ref[d] = fr * bi + fi * br


def _s5_params(lam_re, lam_im, log_step, b_re, b_im, c_re, c_im):
    nd, g, p = lam_re.shape
    gp = g * p
    nb = g // S5_BLOCK_GROUPS
    ls = jnp.repeat(log_step, p, axis=-1).reshape(nd, 1, gp)
    brt = b_re.transpose(0, 3, 1, 2).reshape(nd, S5_GROUP_CH, gp)
    bit = b_im.transpose(0, 3, 1, 2).reshape(nd, S5_GROUP_CH, gp)
    full = lambda *shape: pl.BlockSpec(shape, lambda: (0,) * len(shape))
    pw_re, pw_im, bb_re, bb_im = pl.pallas_call(
        _s5_disc_kernel,
        in_specs=[full(nd, 1, gp), full(nd, 1, gp), full(nd, 1, gp),
                  full(nd, S5_GROUP_CH, gp), full(nd, S5_GROUP_CH, gp)],
        out_specs=[full(nd, len(S5_POWERS), gp), full(nd, len(S5_POWERS), gp),
                   full(nd, S5_GROUP_CH, gp), full(nd, S5_GROUP_CH, gp)],
        out_shape=[jax.ShapeDtypeStruct((nd, len(S5_POWERS), gp), F32)] * 2
        + [jax.ShapeDtypeStruct((nd, S5_GROUP_CH, gp), F32)] * 2,
        name="s5_discretize",
    )(lam_re.reshape(nd, 1, gp), lam_im.reshape(nd, 1, gp), ls, brt, bit)

    eye = jnp.eye(S5_BLOCK_GROUPS, dtype=F32)

    def in_proj(bb):
        bb = bb.reshape(nd, S5_GROUP_CH, nb, S5_BLOCK_GROUPS, p)
        w = jnp.einsum('dhjgp,gk->djghkp', bb, eye)
        return w.reshape(nd, nb, LANES, S5_BLOCK_STATES)

    def out_proj(c):
        c = c.reshape(nd, nb, S5_BLOCK_GROUPS, S5_GROUP_CH, p)
        w = jnp.einsum('djghp,gk->djgpkh', c, eye)
        return w.reshape(nd, nb, S5_BLOCK_STATES, LANES)

    wb = jnp.concatenate([in_proj(bb_re), in_proj(bb_im)], axis=-1).astype(BF16)
    wc = jnp.concatenate([out_proj(c_re), -out_proj(c_im)], axis=-2).astype(BF16)

    rows = jnp.arange(SUBLANES)
    one = (jnp.ones((gp,), F32), jnp.zeros((gp,), F32))
    power = lambda d, m: tuple(pw[d, S5_POWERS.index(m)] for pw in (pw_re, pw_im))
    seg, step = [], []
    for d in range(nd):
        rev = d == 1
        kinds = [jnp.broadcast_to(c, (SUBLANES, gp)) for c in power(d, 1)]
        for sh in (1, 2, 4):
            keep = (rows < SUBLANES - sh) if rev else (rows >= sh)
            kinds += [jnp.where(keep[:, None], c[None, :], 0.0) for c in power(d, sh * S5_SEG)]
        dist = [(SUBLANES - 1 - s) if rev else s for s in range(SUBLANES)]
        per_row = [one if m == 0 else power(d, m * S5_SEG) for m in dist]
        kinds += [jnp.stack([c[part] for c in per_row]) for part in range(2)]
        kinds += [jnp.broadcast_to(c, (SUBLANES, gp)) for c in power(d, SUBLANES * S5_SEG)]
        seg.append(jnp.stack(kinds))
        order = [(S5_SEG - k) if rev else (k + 1) for k in range(S5_SEG)]
        step.append(jnp.stack([jnp.stack([power(d, m)[part] for m in order]) for part in range(2)]))
    coef_seg = jnp.stack(seg).reshape(nd, 12, SUBLANES, nb, S5_BLOCK_STATES).transpose(0, 3, 1, 2, 4)
    coef_step = jnp.stack(step).reshape(nd, 2, S5_SEG, nb, S5_BLOCK_STATES).transpose(0, 3, 1, 2, 4)
    return wb, wc, coef_seg, coef_step


def _s5_scan_kernel(uf_ref, ub_ref, wb_ref, wc_ref, seg_ref, step_ref, yf_ref, yb_ref, bu_ref, carry_ref,
                    *, nb, sq):
    i = pl.program_id(0)
    n = pl.num_programs(0)
    ns = S5_BLOCK_STATES
    cmul = lambda ar, ai, br, bi: (ar * br - ai * bi, ar * bi + ai * br)
    sub = lax.broadcasted_iota(jnp.int32, (SUBLANES, ns), 0)
    for d, (u_ref, y_ref) in enumerate(((uf_ref, yf_ref), (ub_ref, yb_ref))):
        rev = d == 1
        ti = (n - 1 - i) if rev else i
        _, first, last = _tile_seq(ti, S5_TILE, sq)

        @pl.when(last if rev else first)
        def _():
            carry_ref[d] = jnp.zeros(carry_ref.shape[1:], F32)

        ks = range(S5_SEG - 1, -1, -1) if rev else range(S5_SEG)
        edge = 0 if rev else SUBLANES - 1
        for j in range(nb):
            ub = u_ref[:, LANES * j:LANES * (j + 1)].astype(BF16)
            bu_ref[d] = _dot(ub, wb_ref[d, j])
            ar, ai = seg_ref[d, j, 0], seg_ref[d, j, 1]
            sr = jnp.zeros((SUBLANES, ns), F32)
            si = jnp.zeros((SUBLANES, ns), F32)
            for k in ks:
                x = bu_ref[d, SUBLANES * k:SUBLANES * (k + 1), :]
                pr, pi = cmul(ar, ai, sr, si)
                sr, si = pr + x[:, :ns], pi + x[:, ns:]
                bu_ref[d, SUBLANES * k:SUBLANES * (k + 1), :] = jnp.concatenate([sr, si], axis=1)
            for lvl, sh in enumerate((1, 2, 4)):
                s = (SUBLANES - sh) if rev else sh
                pr, pi = cmul(seg_ref[d, j, 2 + 2 * lvl], seg_ref[d, j, 3 + 2 * lvl],
                              pltpu.roll(sr, s, 0), pltpu.roll(si, s, 0))
                sr, si = sr + pr, si + pi
            c0 = carry_ref[d, j]
            cr, ci = c0[:, :ns], c0[:, ns:]
            s1 = (SUBLANES - 1) if rev else 1
            pr, pi = cmul(seg_ref[d, j, 8], seg_ref[d, j, 9], cr, ci)
            in_r = jnp.where(sub == (SUBLANES - 1 - edge), 0.0, pltpu.roll(sr, s1, 0)) + pr
            in_i = jnp.where(sub == (SUBLANES - 1 - edge), 0.0, pltpu.roll(si, s1, 0)) + pi
            pr, pi = cmul(seg_ref[d, j, 10], seg_ref[d, j, 11], cr, ci)
            carry_ref[d, j] = jnp.concatenate(
                [jnp.broadcast_to((sr + pr)[edge:edge + 1, :], (SUBLANES, ns)),
                 jnp.broadcast_to((si + pi)[edge:edge + 1, :], (SUBLANES, ns))], axis=1)
            for k in ks:
                x = bu_ref[d, SUBLANES * k:SUBLANES * (k + 1), :]
                pr, pi = cmul(step_ref[d, j, 0, k:k + 1, :], step_ref[d, j, 1, k:k + 1, :], in_r, in_i)
                bu_ref[d, SUBLANES * k:SUBLANES * (k + 1), :] = jnp.concatenate(
                    [x[:, :ns] + pr, x[:, ns:] + pi], axis=1)
            y_ref[:, LANES * j:LANES * (j + 1)] = _dot(bu_ref[d].astype(BF16), wc_ref[d, j])


def _s5_interleave(x, inverse=False):
    t, w = x.shape
    shape = (t // S5_TILE, S5_SEG, SUBLANES, w) if inverse else (t // S5_TILE, SUBLANES, S5_SEG, w)
    return x.reshape(shape).transpose(0, 2, 1, 3).reshape(t, w)


def _s5_scan_call(u, dm, sq, wb, wc, coef_seg, coef_step):
    t = u.shape[0]
    assert sq.l1 % S5_TILE == 0 and sq.l2 % S5_TILE == 0
    n = t // S5_TILE
    nb = dm.s5w // LANES
    full = lambda a: pl.BlockSpec(a.shape, lambda i: (0,) * a.ndim)
    return pl.pallas_call(
        functools.partial(_s5_scan_kernel, nb=nb, sq=sq),
        grid=(n,),
        in_specs=[pl.BlockSpec((S5_TILE, dm.s5w), lambda i: (i, 0)),
                  pl.BlockSpec((S5_TILE, dm.s5w), lambda i: (n - 1 - i, 0)),
                  full(wb), full(wc), full(coef_seg), full(coef_step)],
        out_specs=[pl.BlockSpec((S5_TILE, dm.s5w), lambda i: (i, 0)),
                   pl.BlockSpec((S5_TILE, dm.s5w), lambda i: (n - 1 - i, 0))],
        out_shape=[jax.ShapeDtypeStruct((t, dm.s5w), F32)] * 2,
        scratch_shapes=[pltpu.VMEM((N_DIR, S5_TILE, 2 * S5_BLOCK_STATES), F32),
                        pltpu.VMEM((N_DIR, nb, SUBLANES, 2 * S5_BLOCK_STATES), F32)],
        compiler_params=_cparams("arbitrary"),
        name="s5_scan",
    )(u, u, wb, wc, coef_seg, coef_step)


def _s5_out_kernel(u_ref, yf_ref, yb_ref, d_ref, w_ref, b_ref, o_ref):
    y = u_ref[...] * d_ref[...] + yf_ref[...] + yb_ref[...]
    y = jax.nn.gelu(y)
    gate = jax.nn.sigmoid(_dot(y.astype(BF16), w_ref[...]) + b_ref[...])
    o_ref[...] = (y * gate).astype(BF16)


def _s5_out_call(u, yf, yb, dm, d_skip, glu_w, glu_b):
    t = u.shape[0]
    w = dm.s5w
    tm = min(512, t)
    row = lambda c: pl.BlockSpec((tm, w), lambda i: (i, c))
    vec = pl.BlockSpec((1, w), lambda i: (0, 0))
    return pl.pallas_call(
        _s5_out_kernel,
        grid=(t // tm,),
        in_specs=[row(0), row(0), row(0), vec, pl.BlockSpec((w, w), lambda i: (0, 0)), vec],
        out_specs=row(0),
        out_shape=jax.ShapeDtypeStruct((t, w), BF16),
        compiler_params=_cparams("parallel"),
        name="s5_out",
    )(u, yf, yb, d_skip.reshape(1, w), glu_w.astype(BF16), glu_b.reshape(1, w))


def _gla_kernel(qf_ref, kf_ref, vf_ref, sf_ref, qb_ref, kb_ref, vb_ref, sb_ref, gu_ref, bias_ref,
                of_ref, ob_ref, st_ref, *, nc, npairs, tb, sq):
    i = pl.program_id(0)
    n = pl.num_programs(0)
    row = lax.broadcasted_iota(jnp.int32, (CHUNK, CHUNK), 0)
    col = lax.broadcasted_iota(jnp.int32, (CHUNK, CHUNK), 1)
    lane = lax.broadcasted_iota(jnp.int32, (CHUNK, LANES), 1)
    first_half = lane < GLA_DK
    row2 = lax.broadcasted_iota(jnp.int32, (2 * CHUNK, LANES), 0)
    col2 = lax.broadcasted_iota(jnp.int32, (2 * CHUNK, LANES), 1)
    scale = GLA_DK ** -0.5
    dirs = ((qf_ref, kf_ref, vf_ref, sf_ref, of_ref), (qb_ref, kb_ref, vb_ref, sb_ref, ob_ref))
    for d, (q_ref, k_ref, v_ref, s_ref, o_ref) in enumerate(dirs):
        rev = d == 1
        ti = (n - 1 - i) if rev else i
        _, first, last = _tile_seq(ti, tb, sq)

        @pl.when(last if rev else first)
        def _():
            st_ref[d] = jnp.zeros(st_ref.shape[1:], F32)

        tri = jnp.where((col >= row) if rev else (col <= row), 1.0, 0.0)
        keep2 = jnp.logical_and((row2 // CHUNK) == (col2 // CHUNK),
                                (col2 % CHUNK > row2 % CHUNK) if rev else (col2 % CHUNK <= row2 % CHUNK))
        for cc in range(nc):
            c = (nc - 1 - cc) if rev else cc
            rows = slice(CHUNK * c, CHUNK * (c + 1))
            z = _dot(s_ref[rows, :], gu_ref[d]) + bias_ref[d]
            log_a = _log_sigmoid(z) * (1.0 / GLA_GATE_NORM)
            b = _dot(tri, log_a, precision=HIGHEST)
            b_last = b[0:1, :] if rev else b[CHUNK - 1:CHUNK, :]
            q = q_ref[rows, :] * scale
            k = k_ref[rows, :]
            q_dec = q * jnp.exp(b)
            k_inv = k * jnp.exp(-b)
            k_dec = k * jnp.exp(b_last - b)
            dec = jnp.exp(b_last)
            for p in range(npairs):
                sl = slice(LANES * p, LANES * (p + 1))
                qp, kip, kdp = q_dec[:, sl], k_inv[:, sl], k_dec[:, sl]
                st = st_ref[d, p]
                q2 = jnp.concatenate([jnp.where(first_half, qp, 0.0), jnp.where(first_half, 0.0, qp)], axis=0)
                kd2 = jnp.concatenate([jnp.where(first_half, kdp, 0.0), jnp.where(first_half, 0.0, kdp)], axis=0)
                attn = jnp.where(keep2, _dot_nt(q2, jnp.concatenate([kip, kip], axis=0)), 0.0)
                vt = v_ref[rows, GLA_DV * 2 * p:GLA_DV * 2 * (p + 1)]
                vt = jnp.concatenate([vt[:, :GLA_DV], vt[:, GLA_DV:]], axis=0).T
                o2 = _dot_nt(jnp.concatenate([q2, attn], axis=1), jnp.concatenate([st, vt], axis=1))
                o_ref[rows, GLA_DV * 2 * p:GLA_DV * (2 * p + 1)] = o2[:CHUNK]
                o_ref[rows, GLA_DV * (2 * p + 1):GLA_DV * (2 * p + 2)] = o2[CHUNK:]
                st_ref[d, p] = st * dec[:, sl] + _dot(vt, kd2)


def _gla_call(proj, dm, sq, gate_up, gate_bias):
    t = proj.shape[0]
    nc = 4
    tb = min(nc * CHUNK, sq.l1, sq.l2)
    nc = tb // CHUNK
    n = t // tb
    npairs = dm.gla_h // 2
    qk_w = dm.gla_h * GLA_DK
    gu = jnp.zeros((N_DIR, LANES, qk_w), F32)
    for d in range(N_DIR):
        gu = gu.at[d, d * GLA_LOW_RANK:(d + 1) * GLA_LOW_RANK, :].set(gate_up[d])
    bias = gate_bias.reshape(N_DIR, 1, qk_w)

    def specs(tile):
        return [pl.BlockSpec((tb, qk_w), lambda i: (tile(i), dm.o_gq // qk_w)),
                pl.BlockSpec((tb, qk_w), lambda i: (tile(i), dm.o_gk // qk_w)),
                pl.BlockSpec((tb, dm.gla_w), lambda i: (tile(i), dm.o_gv // dm.gla_w)),
                pl.BlockSpec((tb, LANES), lambda i: (tile(i), dm.o_small // LANES))]

    fwd = lambda i: i
    bwd = lambda i: n - 1 - i
    full = lambda a: pl.BlockSpec(a.shape, lambda i: (0,) * a.ndim)
    return pl.pallas_call(
        functools.partial(_gla_kernel, nc=nc, npairs=npairs, tb=tb, sq=sq),
        grid=(n,),
        in_specs=specs(fwd) + specs(bwd) + [full(gu), full(bias)],
        out_specs=[pl.BlockSpec((tb, dm.gla_w), lambda i: (i, 0)),
                   pl.BlockSpec((tb, dm.gla_w), lambda i: (n - 1 - i, 0))],
        out_shape=[jax.ShapeDtypeStruct((t, dm.gla_w), F32)] * 2,
        scratch_shapes=[pltpu.VMEM((N_DIR, npairs, GLA_DV, LANES), F32)],
        compiler_params=_cparams("arbitrary"),
        name="gla_chunks",
    )(*([proj] * 8), gu, bias)


def _gla_out_kernel(of_ref, ob_ref, g_ref, gain_ref, o_ref, *, heads):
    for h in range(heads):
        sl = slice(GLA_DV * h, GLA_DV * (h + 1))
        o = of_ref[:, sl] + ob_ref[:, sl]
        r = lax.rsqrt(jnp.mean(o * o, axis=-1, keepdims=True) + NORM_EPS)
        o_ref[:, sl] = (o * r * gain_ref[...] * _silu(g_ref[:, sl])).astype(BF16)


def _gla_out_call(proj, of, ob, dm, gain):
    t = proj.shape[0]
    w = dm.gla_w
    tm = min(512, t)
    row = lambda c: pl.BlockSpec((tm, w), lambda i: (i, c))
    return pl.pallas_call(
        functools.partial(_gla_out_kernel, heads=dm.gla_h),
        grid=(t // tm,),
        in_specs=[row(0), row(0), row(dm.o_gg // w), pl.BlockSpec((1, GLA_DV), lambda i: (0, 0))],
        out_specs=row(0),
        out_shape=jax.ShapeDtypeStruct((t, w), BF16),
        compiler_params=_cparams("parallel"),
        name="gla_out",
    )(of, ob, proj, gain.reshape(1, GLA_DV))


GDN_BETA_LANE = N_DIR * GLA_LOW_RANK
GDN_HEADS_PER_STEP = 3


def _paired_unit_triangular_solve(a, xs, upper, lo):
    nblk = CHUNK // SUBLANES
    a_blk = [a[SUBLANES * i:SUBLANES * (i + 1), :] for i in range(nblk)]
    x_blk = [[x[SUBLANES * i:SUBLANES * (i + 1), :] for i in range(nblk)] for x in xs]
    order = range(CHUNK - 1, 0, -1) if upper else range(CHUNK - 1)
    for j in order:
        bj, rj = divmod(j, SUBLANES)
        idx = jnp.where(lo, j, CHUNK + j)
        rows = [xb[bj][rj:rj + 1, :] for xb in x_blk]
        for i in (range(bj + 1) if upper else range(bj, nblk)):
            col = jnp.take_along_axis(a_blk[i], idx, axis=1)
            for xb, x_row in zip(x_blk, rows):
                xb[i] = xb[i] - col * x_row
    return [jnp.concatenate(xb, axis=0) for xb in x_blk]


def _gdn_wy_kernel(x_ref, prev_ref, next_ref, s_ref, cw_ref, alog_ref, dtb_ref,
                   u_ref, w_ref, qd_ref, kd_ref, qk_ref, gc_ref,
                   ext_ref, q_ref, k_ref, v_ref, beta_ref, gcx_ref, gct_ref, *, tm, heads, sq):
    i = pl.program_id(0)
    _, first, last = _tile_seq(i, tm, sq)
    halo = SUBLANES
    pad = GDN_CONV // 2
    w = heads * GDN_DK
    ext_ref[0:halo, :] = jnp.where(first, 0.0, prev_ref[...])
    ext_ref[halo:halo + tm, :] = x_ref[...]
    ext_ref[halo + tm:, :] = jnp.where(last, 0.0, next_ref[...])
    acc = cw_ref[0:1, :] * ext_ref[pl.ds(halo - pad, tm), :]
    for j in range(1, GDN_CONV):
        acc = acc + cw_ref[j:j + 1, :] * ext_ref[pl.ds(halo - pad + j, tm), :]
    y = _silu(acc)
    for h in range(heads):
        qh = y[:, GDN_DK * h:GDN_DK * (h + 1)]
        kh = y[:, w + GDN_DK * h:w + GDN_DK * (h + 1)]
        q_ref[h] = qh * lax.rsqrt(jnp.sum(qh * qh, axis=-1, keepdims=True) + NORM_EPS) * (GDN_DK ** -0.5)
        k_ref[h] = kh * lax.rsqrt(jnp.sum(kh * kh, axis=-1, keepdims=True) + NORM_EPS)
        v_ref[h] = y[:, 2 * w + GDN_DV * h:2 * w + GDN_DV * (h + 1)]
    sm = s_ref[...]
    beta = jax.nn.sigmoid(sm)
    g = -jnp.exp(alog_ref[...]) * _softplus(sm + dtb_ref[...])
    row = lax.broadcasted_iota(jnp.int32, (tm, tm), 0)
    col = lax.broadcasted_iota(jnp.int32, (tm, tm), 1)
    same = (row // CHUNK) == (col // CHUNK)
    a_lane = GDN_BETA_LANE + N_DIR * heads
    for d in range(N_DIR):
        tri = jnp.where(jnp.logical_and(same, (col >= row) if d == 1 else (col <= row)), 1.0, 0.0)
        gc = _dot(tri, g, precision=HIGHEST)
        gc_ref[d] = gc
        for h in range(heads):
            lb = GDN_BETA_LANE + d * heads + h
            la = a_lane + d * heads + h
            beta_ref[d, h] = jnp.broadcast_to(beta[:, lb:lb + 1], (tm, LANES))
            gcx_ref[d, h] = jnp.broadcast_to(gc[:, la:la + 1], (tm, LANES))
        for c in range(tm // CHUNK):
            gcm = gc[CHUNK * c:CHUNK * (c + 1), :]
            zero = jnp.zeros_like(gcm)
            gct_ref[d, c] = jnp.concatenate([gcm, zero] if c == 0 else [zero, gcm], axis=0).T

    row = lax.broadcasted_iota(jnp.int32, (CHUNK, LANES), 0)
    lane = lax.broadcasted_iota(jnp.int32, (CHUNK, LANES), 1)
    lo = lane < CHUNK
    t_col = jnp.where(lo, lane, lane - CHUNK)
    lo_idx = lax.broadcasted_iota(jnp.int32, (SUBLANES, LANES), 1) < CHUNK

    def halves(x):
        return x[:CHUNK], x[CHUNK:]

    def pack(x):
        top, bot = halves(x)
        return [jnp.where(lo, top, pltpu.roll(bot, CHUNK, 1)),
                jnp.where(lo, pltpu.roll(top, CHUNK, 1), bot)]

    def unpack(p1, p2):
        return jnp.concatenate([jnp.where(lo, p1, pltpu.roll(p2, CHUNK, 1)),
                                jnp.where(lo, pltpu.roll(p1, CHUNK, 1), p2)], axis=0)

    def diag_blocks(x):
        top, bot = halves(x)
        return jnp.where(lo, top, bot)

    def body(hp, carry):
        loaded = []
        for j in range(GDN_HEADS_PER_STEP):
            h = hp * GDN_HEADS_PER_STEP + j
            per_dir = [(beta_ref[d, h], gcx_ref[d, h],
                        gct_ref[d, 0, pl.ds(a_lane + d * heads + h, 1), :]
                        + gct_ref[d, 1, pl.ds(a_lane + d * heads + h, 1), :]) for d in range(N_DIR)]
            loaded.append((h, q_ref[h], k_ref[h], v_ref[h], per_dir))
        stores = []
        for h, q, k, v, per_dir in loaded:
            gram = diag_blocks(_dot_nt(k, k))
            qkt = diag_blocks(_dot_nt(q, k))
            for d, (bt, gc, gc_row) in enumerate(per_dir):
                rev = d == 1
                incl = (t_col >= row) if rev else (t_col <= row)
                strict = (t_col > row) if rev else (t_col < row)
                decay = jnp.exp(jnp.where(incl, diag_blocks(gc) - gc_row, -jnp.inf))
                a = jnp.where(strict, diag_blocks(bt) * gram * decay, 0.0)
                kb = k * bt
                xs = _paired_unit_triangular_solve(a, pack(v * bt) + pack(kb * jnp.exp(gc)), rev, lo_idx)
                e = 0 if rev else CHUNK - 1
                gc_last = jnp.concatenate(
                    [jnp.broadcast_to(gc[e:e + 1, :], (CHUNK, LANES)),
                     jnp.broadcast_to(gc[CHUNK + e:CHUNK + e + 1, :], (CHUNK, LANES))], axis=0)
                qk = jnp.where(strict if rev else incl, qkt * decay, 0.0)
                stores.append((d, h, unpack(xs[0], xs[1]), unpack(xs[2], xs[3]).astype(BF16),
                               (q * jnp.exp(gc)).astype(BF16), (k * jnp.exp(gc_last - gc)).T.astype(BF16),
                               qk.astype(BF16)))
        for d, h, u, w, qd, kd, qk in stores:
            u_ref[d, h] = u
            w_ref[d, h] = w
            qd_ref[d, h] = qd
            kd_ref[d, h] = kd
            qk_ref[d, h] = qk
        return carry

    lax.fori_loop(0, heads // GDN_HEADS_PER_STEP, body, 0)


def _gdn_wy_call(proj, dm, sq, conv_w, a_log, dt_bias):
    t = proj.shape[0]
    heads = dm.gdn_h
    tm = 2 * CHUNK
    assert sq.l1 % tm == 0 and sq.l2 % tm == 0 and heads % GDN_HEADS_PER_STEP == 0
    n = t // tm
    wq = 3 * dm.gdn_w
    hb = tm // SUBLANES
    a_lane = GDN_BETA_LANE + N_DIR * heads
    alog = jnp.zeros((1, LANES), F32).at[0, a_lane:a_lane + N_DIR * heads].set(a_log.reshape(-1))
    dtb = jnp.zeros((1, LANES), F32).at[0, a_lane:a_lane + N_DIR * heads].set(dt_bias.reshape(-1))
    cw = conv_w.reshape(GDN_CONV, wq)
    cb = dm.o_qkv // wq
    nhb = t // SUBLANES
    dhm = lambda dt: jax.ShapeDtypeStruct((N_DIR, heads, t, LANES), dt)
    dhm_spec = pl.BlockSpec((N_DIR, heads, tm, LANES), lambda i: (0, 0, i, 0))
    return pl.pallas_call(
        functools.partial(_gdn_wy_kernel, tm=tm, heads=heads, sq=sq),
        grid=(n,),
        in_specs=[pl.BlockSpec((tm, wq), lambda i: (i, cb)),
                  pl.BlockSpec((SUBLANES, wq), lambda i: (jnp.maximum(i * hb - 1, 0), cb)),
                  pl.BlockSpec((SUBLANES, wq), lambda i: (jnp.minimum((i + 1) * hb, nhb - 1), cb)),
                  pl.BlockSpec((tm, LANES), lambda i: (i, dm.o_small // LANES)),
                  pl.BlockSpec((GDN_CONV, wq), lambda i: (0, 0)),
                  pl.BlockSpec((1, LANES), lambda i: (0, 0)),
                  pl.BlockSpec((1, LANES), lambda i: (0, 0))],
        out_specs=[dhm_spec] * 4
        + [pl.BlockSpec((N_DIR, heads, CHUNK, LANES), lambda i: (0, 0, i, 0)),
           pl.BlockSpec((N_DIR, tm, LANES), lambda i: (0, i, 0))],
        out_shape=[dhm(F32)] + [dhm(BF16)] * 3
        + [jax.ShapeDtypeStruct((N_DIR, heads, t // 2, LANES), BF16),
           jax.ShapeDtypeStruct((N_DIR, t, LANES), F32)],
        scratch_shapes=[pltpu.VMEM((tm + 2 * SUBLANES, wq), F32)]
        + [pltpu.VMEM((heads, tm, LANES), F32)] * 3
        + [pltpu.VMEM((N_DIR, heads, tm, LANES), F32)] * 2
        + [pltpu.VMEM((N_DIR, tm // CHUNK, LANES, LANES), F32)],
        compiler_params=_cparams("parallel"),
        name="gdn_wy",
    )(proj, proj, proj, proj, cw, alog, dtb)


def _gdn_scan_kernel(uf_ref, wf_ref, qdf_ref, kdf_ref, qkf_ref, gcf_ref,
                     ub_ref, wb_ref, qdb_ref, kdb_ref, qkb_ref, gcb_ref,
                     of_ref, ob_ref, s_ref, *, nc, heads, tb, sq):
    i = pl.program_id(0)
    n = pl.num_programs(0)
    a_lane = GDN_BETA_LANE + N_DIR * heads
    dirs = ((uf_ref, wf_ref, qdf_ref, kdf_ref, qkf_ref, gcf_ref, of_ref),
            (ub_ref, wb_ref, qdb_ref, kdb_ref, qkb_ref, gcb_ref, ob_ref))
    for d in range(N_DIR):
        rev = d == 1
        ti = (n - 1 - i) if rev else i
        _, first, last = _tile_seq(ti, tb, sq)

        @pl.when(last if rev else first)
        def _():
            s_ref[d] = jnp.zeros(s_ref.shape[1:], F32)

    lo = lax.broadcasted_iota(jnp.int32, (CHUNK + GDN_DK, LANES), 1) < CHUNK
    zeros = jnp.zeros((CHUNK, GDN_DV), BF16)
    for cc in range(nc):
        for d, (u_ref, w_ref, qd_ref, kd_ref, qk_ref, gc_ref, o_ref) in enumerate(dirs):
            rev = d == 1
            c = (nc - 1 - cc) if rev else cc
            rows = slice(CHUNK * c, CHUNK * (c + 1))
            e = CHUNK * c + (0 if rev else CHUNK - 1)
            for h in range(heads):
                state = s_ref[d, h]
                wq = _dot(jnp.concatenate([w_ref[0, h, rows, :], qd_ref[0, h, rows, :]], axis=0),
                          state.astype(BF16))
                v_new = (u_ref[0, h, rows, :] - wq[:CHUNK]).astype(BF16)
                both = jnp.concatenate([qk_ref[0, h, CHUNK * (c // 2):CHUNK * (c // 2 + 1), :],
                                        kd_ref[0, h, 2 * CHUNK * (c // 2):2 * CHUNK * (c // 2 + 1), :]], axis=0)
                if c % 2 == 0:
                    res = _dot(jnp.where(lo, both, 0), jnp.concatenate([v_new, zeros], axis=0))
                else:
                    res = _dot(jnp.where(lo, 0, both), jnp.concatenate([zeros, v_new], axis=0))
                o_ref[h, rows, :] = wq[CHUNK:] + res[:CHUNK]
                la = a_lane + d * heads + h
                dec = jnp.exp(gc_ref[0, e:e + 1, la:la + 1])
                s_ref[d, h] = state * dec + res[CHUNK:]


def _gdn_scan_call(u, w, qd, kd, qk, gc, dm, sq):
    _, heads, t, _ = u.shape
    nc = 2
    tb = min(nc * CHUNK, sq.l1, sq.l2)
    nc = tb // CHUNK
    n = t // tb
    fwd = lambda i: i
    bwd = lambda i: n - 1 - i
    hm = lambda tile: pl.BlockSpec((heads, tb, LANES), lambda i: (0, tile(i), 0))
    dhm = lambda d, tile: pl.BlockSpec((1, heads, tb, LANES), lambda i: (d, 0, tile(i), 0))
    cm = lambda d, tile: pl.BlockSpec((1, tb, LANES), lambda i: (d, tile(i), 0))
    qkm = lambda d, tile: pl.BlockSpec((1, heads, tb // 2, LANES), lambda i: (d, 0, tile(i), 0))
    per_dir = lambda d, tile: [dhm(d, tile)] * 4 + [qkm(d, tile), cm(d, tile)]
    return pl.pallas_call(
        functools.partial(_gdn_scan_kernel, nc=nc, heads=heads, tb=tb, sq=sq),
        grid=(n,),
        in_specs=per_dir(0, fwd) + per_dir(1, bwd),
        out_specs=[hm(fwd), hm(bwd)],
        out_shape=[jax.ShapeDtypeStruct((heads, t, LANES), F32)] * 2,
        scratch_shapes=[pltpu.VMEM((N_DIR, heads, GDN_DK, GDN_DV), F32)],
        compiler_params=_cparams("arbitrary"),
        name="gdn_scan",
    )(u, w, qd, kd, qk, gc, u, w, qd, kd, qk, gc)


def _gdn_out_kernel(of_ref, ob_ref, z_ref, gain_ref, o_ref, *, heads):
    for h in range(heads):
        sl = slice(GDN_DV * h, GDN_DV * (h + 1))
        o = of_ref[h] + ob_ref[h]
        r = lax.rsqrt(jnp.mean(o * o, axis=-1, keepdims=True) + NORM_EPS)
        o_ref[:, sl] = (o * r * gain_ref[...] * _silu(z_ref[:, sl])).astype(BF16)


def _gdn_out_call(proj, of, ob, dm, gain):
    t = proj.shape[0]
    w = dm.gdn_w
    heads = dm.gdn_h
    tm = min(512, t)
    hm = pl.BlockSpec((heads, tm, LANES), lambda i: (0, i, 0))
    return pl.pallas_call(
        functools.partial(_gdn_out_kernel, heads=heads),
        grid=(t // tm,),
        in_specs=[hm, hm, pl.BlockSpec((tm, w), lambda i: (i, dm.o_z // w)),
                  pl.BlockSpec((1, GDN_DV), lambda i: (0, 0))],
        out_specs=pl.BlockSpec((tm, w), lambda i: (i, 0)),
        out_shape=jax.ShapeDtypeStruct((t, w), BF16),
        compiler_params=_cparams("parallel"),
        name="gdn_out",
    )(of, ob, proj, gain.reshape(1, GDN_DV))


def _ffn_act_kernel(g_ref, gp_ref, gn_ref, v_ref, vp_ref, vn_ref, cwg_ref, cwv_ref, o_ref, *, tm, sq):
    i = pl.program_id(0)
    _, first, last = _tile_seq(i, tm, sq)
    sub = min(LANES, tm)
    row = lax.broadcasted_iota(jnp.int32, (2 * sub, sub), 0)
    col = lax.broadcasted_iota(jnp.int32, (2 * sub, sub), 1)
    shifts = jnp.where(col == jnp.where(row < sub, row - 1, row - sub + 1), 1.0, 0.0).astype(BF16)
    edge = lax.broadcasted_iota(jnp.int32, (SUBLANES, 1), 0)

    def conv(x_ref, prev_ref, next_ref, cw_ref):
        xb = x_ref[...]
        xf = xb.astype(F32)
        prevs, nexts = [], []
        for s in range(tm // sub):
            r0 = sub * s
            xs = _dot(shifts, xb[r0:r0 + sub])
            before = (jnp.where(first, 0.0, prev_ref[BF16_ROWS - 1:BF16_ROWS, :].astype(F32)) if s == 0
                      else xf[r0 - 1:r0])
            after = (jnp.where(last, 0.0, next_ref[0:1, :].astype(F32)) if r0 + sub == tm
                     else xf[r0 + sub:r0 + sub + 1])
            prevs += [xs[0:SUBLANES] + jnp.where(edge == 0, before, 0.0), xs[SUBLANES:sub]]
            nexts += [xs[sub:2 * sub - SUBLANES],
                      xs[2 * sub - SUBLANES:] + jnp.where(edge == SUBLANES - 1, after, 0.0)]
        x_prev = jnp.concatenate(prevs, axis=0)
        x_next = jnp.concatenate(nexts, axis=0)
        return cw_ref[0:1, :] * x_prev + cw_ref[1:2, :] * xf + cw_ref[2:3, :] * x_next

    gate = conv(g_ref, gp_ref, gn_ref, cwg_ref)
    val = conv(v_ref, vp_ref, vn_ref, cwv_ref)
    o_ref[...] = (_silu(gate) * val).astype(BF16)


def _ffn_act_call(h1, dm, sq, cw_gate, cw_val):
    t = h1.shape[0]
    assert FFN_CONV == 3
    f = dm.dffp
    tm = min(256, sq.l1, sq.l2)
    tc = min(MM_TILE, f)
    hb = tm // BF16_ROWS
    nhb = t // BF16_ROWS
    nv = f // tc
    main = lambda off: pl.BlockSpec((tm, tc), lambda i, j: (i, j + off))
    prev = lambda off: pl.BlockSpec((BF16_ROWS, tc), lambda i, j: (jnp.maximum(i * hb - 1, 0), j + off))
    nxt = lambda off: pl.BlockSpec((BF16_ROWS, tc), lambda i, j: (jnp.minimum((i + 1) * hb, nhb - 1), j + off))
    cws = pl.BlockSpec((FFN_CONV, tc), lambda i, j: (0, j))
    return pl.pallas_call(
        functools.partial(_ffn_act_kernel, tm=tm, sq=sq),
        grid=(t // tm, f // tc),
        in_specs=[main(0), prev(0), nxt(0), main(nv), prev(nv), nxt(nv), cws, cws],
        out_specs=pl.BlockSpec((tm, tc), lambda i, j: (i, j)),
        out_shape=jax.ShapeDtypeStruct((t, f), BF16),
        compiler_params=_cparams("parallel", "parallel"),
        name="ffn_conv_swiglu",
    )(h1, h1, h1, h1, h1, h1, cw_gate, cw_val)


def _pad_cols(w, n):
    return jnp.pad(w, ((0, 0), (0, n - w.shape[1])))


def _layout_w_in(w_in, dm):
    qk = dm.gla_h * GLA_DK
    widths = (dm.s5w, qk, qk, dm.gla_w, dm.gla_w, N_DIR * GLA_LOW_RANK, 3 * dm.gdn_w, dm.gdn_w,
              N_DIR * dm.gdn_h, N_DIR * dm.gdn_h)
    splits = np.cumsum(widths)[:-1]
    s5_u, gq, gk, gv, gg, glr, qkv, z, beta, a = jnp.split(w_in, splits, axis=1)
    small = _pad_cols(jnp.concatenate([glr, beta, a], axis=1), LANES)
    rows = w_in.shape[0]
    parts, cur = [], 0
    for off, blk in sorted(((dm.o_qkv, qkv), (dm.o_z, z), (dm.o_gv, gv), (dm.o_gg, gg), (dm.o_gq, gq),
                            (dm.o_gk, gk), (dm.o_small, small), (dm.o_s5, s5_u)), key=lambda p: p[0]):
        if off > cur:
            parts.append(jnp.zeros((rows, off - cur), BF16))
        parts.append(blk.astype(BF16))
        cur = off + blk.shape[1]
    if cur < dm.n_in_pad:
        parts.append(jnp.zeros((rows, dm.n_in_pad - cur), BF16))
    return jnp.concatenate(parts, axis=1)


def _layer(x, sq, dm, mod, norm_gains, w_in, s5, gla, gdn, w_out, ffn_up, ffn_conv, ffn_down,
           pending):
    d = dm.d
    mod3 = mod.reshape(mod.shape[0], 1, 6 * d)
    gains3 = norm_gains.reshape(4, 1, d)
    if pending is None:
        (h,) = _resnorm_call(x, sq, norm=(mod3, 1, 0, gains3, 0))
    else:
        x, h = _resnorm_call(x, sq, res=pending, norm=(mod3, 1, 0, gains3, 0))
    proj = _mm_call(h, _layout_w_in(w_in, dm), F32, "in_proj")

    u_s5 = _s5_interleave(proj[:, dm.o_s5:dm.o_s5 + dm.s5w])
    yf, yb = _s5_scan_call(u_s5, dm, sq, *_s5_params(*s5[:7]))
    y_s5 = _s5_interleave(_s5_out_call(u_s5, yf, yb, dm, s5[7], s5[8], s5[9]), inverse=True)

    of, ob = _gla_call(proj, dm, sq, gla[0], gla[1])
    y_gla = _gla_out_call(proj, of, ob, dm, gla[2])

    of, ob = _gdn_scan_call(*_gdn_wy_call(proj, dm, sq, gdn[0], gdn[1], gdn[2]), dm, sq)
    y_gdn = _gdn_out_call(proj, of, ob, dm, gdn[3])

    mixed = _mm_call(jnp.concatenate([y_s5, y_gla, y_gdn], axis=1), w_out.astype(BF16), F32, "out_proj")
    x, h = _resnorm_call(x, sq, res=(mixed, mod3, 2, gains3, 1), norm=(mod3, 4, 3, gains3, 2))
    f, fp = dm.dff, dm.dffp
    up_b = ffn_up.astype(BF16)
    up = jnp.concatenate([_pad_cols(up_b[:, :f], fp), _pad_cols(up_b[:, f:], fp)], axis=1)
    cw = ffn_conv.reshape(FFN_CONV, 2 * f)
    h1 = _mm_call(h, up, BF16, "ffn_up")
    act = _ffn_act_call(h1, dm, sq, _pad_cols(cw[:, :f], fp), _pad_cols(cw[:, f:], fp))
    down = jnp.pad(ffn_down.astype(BF16), ((0, fp - f), (0, 0)))
    f_out = _mm_call(act, down, F32, "ffn_down")
    return x, (f_out, mod3, 5, gains3, 3)


def kernel(x_prompt, x_sample, c_prompt, c_sample, ada_w, ada_b, norm_gains, w_in, s5_lambda_re, s5_lambda_im, s5_log_step, s5_b_re, s5_b_im, s5_c_re, s5_c_im, s5_d, s5_glu_w, s5_glu_b, gla_gate_up, gla_gate_bias, gla_norm, gdn_conv, gdn_a_log, gdn_dt_bias, gdn_norm, w_out, ffn_up, ffn_conv, ffn_down):
    b1, l1, d = x_prompt.shape
    b2, l2, _ = x_sample.shape
    sq = _Seq(b1, l1, b2, l2)
    dm = _make_dims(d)
    depth = ada_w.shape[0]
    x = jnp.concatenate([x_prompt.reshape(b1 * l1, d), x_sample.reshape(b2 * l2, d)], axis=0)
    c = jnp.concatenate([c_prompt, c_sample], axis=0)
    c = jnp.pad(c, ((0, _round_up(sq.nseq, 2 * SUBLANES) - sq.nseq), (0, 0)))
    mod = _mod_call(c, ada_w, ada_b)
    pending = None
    for l in range(depth):
        s5 = (s5_lambda_re[l], s5_lambda_im[l], s5_log_step[l], s5_b_re[l], s5_b_im[l], s5_c_re[l],
              s5_c_im[l], s5_d[l], s5_glu_w[l], s5_glu_b[l])
        gla = (gla_gate_up[l], gla_gate_bias[l], gla_norm[l])
        gdn = (gdn_conv[l], gdn_a_log[l], gdn_dt_bias[l], gdn_norm[l])
        x, pending = _layer(x, sq, dm, mod[l], norm_gains[l], w_in[l], s5, gla, gdn, w_out[l],
                            ffn_up[l], ffn_conv[l], ffn_down[l], pending)
    (x,) = _resnorm_call(x, sq, res=pending)
    return (x[:b1 * l1].reshape(b1, l1, d), x[b1 * l1:].reshape(b2, l2, d))
```

```python
import functools
import math
from typing import NamedTuple

import jax
import jax.numpy as jnp
import numpy as np
from jax import lax
from jax.experimental import pallas as pl
from jax.experimental.pallas import tpu as pltpu

F32 = jnp.float32
BF16 = jnp.bfloat16
HIGHEST = lax.Precision.HIGHEST

NORM_EPS = 1e-6
CHUNK = 64
LANES = 128
SUBLANES = 8
BF16_ROWS = 16
N_DIR = 2
S5_GROUP_CH = 16
S5_STATE = 64
S5_BLOCK_GROUPS = LANES // S5_GROUP_CH
S5_BLOCK_STATES = S5_BLOCK_GROUPS * S5_STATE
S5_TILE = 256
S5_SEG = S5_TILE // SUBLANES
S5_POWERS = tuple(range(1, S5_SEG + 1)) + tuple(S5_SEG * s for s in range(2, SUBLANES + 1))
GLA_DV = 128
GLA_DK = 64
GLA_LOW_RANK = 16
GLA_GATE_NORM = 16.0
GDN_DK = 128
GDN_DV = 128
GDN_CONV = 5
FFN_CONV = 3
VMEM_LIMIT_BYTES = 56 * 1024 * 1024
MM_TILE = 1024


class _Seq(NamedTuple):
    b1: int
    l1: int
    b2: int
    l2: int

    @property
    def tokens(self):
        return self.b1 * self.l1 + self.b2 * self.l2

    @property
    def nseq(self):
        return self.b1 + self.b2


class _Dims(NamedTuple):
    d: int
    s5w: int
    gla_w: int
    gla_h: int
    gdn_w: int
    gdn_h: int
    dff: int
    dffp: int
    o_qkv: int
    o_z: int
    o_gv: int
    o_gg: int
    o_gq: int
    o_gk: int
    o_small: int
    o_s5: int
    n_in_pad: int


def _round_up(x, m):
    return (x + m - 1) // m * m


def _make_dims(d):
    s5w = d // 4
    gla_w = 3 * d // 8
    gla_h = gla_w // GLA_DV
    gdn_w = d - s5w - gla_w
    gdn_h = gdn_w // GDN_DV
    dff = 256 * ((8 * d // 3 + 255) // 256)
    dffp = _round_up(dff, MM_TILE)
    segs = [3 * gdn_w, gdn_w, gla_w, gla_w, gla_h * GLA_DK, gla_h * GLA_DK, LANES, s5w]
    offs = []
    cur = 0
    for w in segs:
        cur = _round_up(cur, w)
        offs.append(cur)
        cur += w
    return _Dims(d, s5w, gla_w, gla_h, gdn_w, gdn_h, dff, dffp, *offs, _round_up(cur, MM_TILE))


def _tile_seq(i, tm, sq):
    p1 = sq.l1 // tm
    p2 = sq.l2 // tm
    n1 = sq.b1 * p1
    in1 = i < n1
    j = jnp.maximum(i - n1, 0)
    seq = jnp.where(in1, i // p1, sq.b1 + j // p2)
    pos = jnp.where(in1, i % p1, j % p2)
    last = jnp.where(in1, p1 - 1, p2 - 1)
    return seq, pos == 0, pos == last


def _cparams(*sem):
    return pltpu.CompilerParams(dimension_semantics=sem, vmem_limit_bytes=VMEM_LIMIT_BYTES)


def _dot(a, b, **kw):
    return jnp.dot(a, b, preferred_element_type=F32, **kw)


def _dot_nt(a, b, **kw):
    return lax.dot_general(a, b, (((1,), (1,)), ((), ())), preferred_element_type=F32, **kw)


def _dot_tn(a, b, **kw):
    return lax.dot_general(a, b, (((0,), (0,)), ((), ())), preferred_element_type=F32, **kw)


def _softplus(x):
    return jnp.maximum(x, 0.0) + jnp.log1p(jnp.exp(-jnp.abs(x)))


def _log_sigmoid(x):
    return jnp.minimum(x, 0.0) - jnp.log1p(jnp.exp(-jnp.abs(x)))


def _silu(x):
    return x * jax.nn.sigmoid(x)


def _mod_kernel(c_ref, w_ref, b_ref, o_ref):
    a = _silu(c_ref[...]).astype(BF16)
    o_ref[0] = _dot(a, w_ref[0].astype(BF16)) + b_ref[0]


def _mod_call(c_rows, ada_w, ada_b):
    nl, d, n = ada_w.shape
    r = c_rows.shape[0]
    tn = min(512, n)
    return pl.pallas_call(
        _mod_kernel,
        grid=(nl, n // tn),
        in_specs=[pl.BlockSpec((r, d), lambda l, j: (0, 0)),
                  pl.BlockSpec((1, d, tn), lambda l, j: (l, 0, j)),
                  pl.BlockSpec((1, 1, tn), lambda l, j: (l, 0, j))],
        out_specs=pl.BlockSpec((1, r, tn), lambda l, j: (l, 0, j)),
        out_shape=jax.ShapeDtypeStruct((nl, r, n), F32),
        compiler_params=_cparams("parallel", "parallel"),
        name="adaln_mod",
    )(c_rows, ada_w, ada_b.reshape(nl, 1, n))


def _resnorm_kernel(*refs, has_res, has_norm):
    it = iter(refs)
    x_ref = next(it)
    if has_res:
        m_ref, gate_ref, ga_ref = next(it), next(it), next(it)
    if has_norm:
        gb_ref, scale_ref, shift_ref = next(it), next(it), next(it)
    if has_res:
        xo_ref = next(it)
    if has_norm:
        h_ref = next(it)
    x = x_ref[...]
    if has_res:
        m = m_ref[...]
        r = lax.rsqrt(jnp.mean(m * m, axis=-1, keepdims=True) + NORM_EPS)
        x = x + gate_ref[0] * (m * r * ga_ref[0])
        xo_ref[...] = x
    if has_norm:
        r = lax.rsqrt(jnp.mean(x * x, axis=-1, keepdims=True) + NORM_EPS)
        h = (x * r * gb_ref[0]) * (1.0 + scale_ref[0]) + shift_ref[0]
        h_ref[...] = h.astype(BF16)


def _resnorm_call(x, sq, *, res=None, norm=None):
    t, d = x.shape
    tm = min(256, sq.l1, sq.l2)
    row = pl.BlockSpec((tm, d), lambda i: (i, 0))

    def mod_spec(col):
        return pl.BlockSpec((1, 1, d), lambda i: (_tile_seq(i, tm, sq)[0], 0, col))

    def gain_spec(r):
        return pl.BlockSpec((1, 1, d), lambda i: (r, 0, 0))

    args, in_specs, out_shape, out_specs = [x], [row], [], []
    if res is not None:
        m, mod3, gate_col, gains3, gain_row = res
        args += [m, mod3, gains3]
        in_specs += [row, mod_spec(gate_col), gain_spec(gain_row)]
        out_shape.append(jax.ShapeDtypeStruct((t, d), F32))
        out_specs.append(row)
    if norm is not None:
        mod3, scale_col, shift_col, gains3, gain_row = norm
        args += [gains3, mod3, mod3]
        in_specs += [gain_spec(gain_row), mod_spec(scale_col), mod_spec(shift_col)]
        out_shape.append(jax.ShapeDtypeStruct((t, d), BF16))
        out_specs.append(row)
    return pl.pallas_call(
        functools.partial(_resnorm_kernel, has_res=res is not None, has_norm=norm is not None),
        grid=(t // tm,),
        in_specs=in_specs, out_specs=out_specs, out_shape=out_shape,
        compiler_params=_cparams("parallel"),
        name="resnorm",
    )(*args)


def _mm_kernel(x_ref, w_ref, o_ref):
    o_ref[...] = _dot(x_ref[...], w_ref[...]).astype(o_ref.dtype)


def _mm_call(x, w, out_dtype, name):
    m, k = x.shape
    n = w.shape[1]
    tile = MM_TILE if k <= 4 * MM_TILE else MM_TILE // 2
    tm = min(tile, m)
    tn = min(tile, n)
    return pl.pallas_call(
        _mm_kernel,
        grid=(m // tm, n // tn),
        in_specs=[pl.BlockSpec((tm, k), lambda i, j: (i, 0)),
                  pl.BlockSpec((k, tn), lambda i, j: (0, j))],
        out_specs=pl.BlockSpec((tm, tn), lambda i, j: (i, j)),
        out_shape=jax.ShapeDtypeStruct((m, n), out_dtype),
        compiler_params=_cparams("parallel", "arbitrary"),
        name=name,
    )(x, w)


def _s5_disc_kernel(lr_ref, li_ref, ls_ref, brt_ref, bit_ref, pw_re_ref, pw_im_ref, bb_re_ref, bb_im_ref):
    for d in range(N_DIR):
        lr = lr_ref[d]
        li = li_ref[d]
        delta = jnp.exp(ls_ref[d])
        for r, m in enumerate(S5_POWERS):
            mag = jnp.exp(lr * delta * float(m))
            th = li * delta * float(m)
            pw_re_ref[d, r:r + 1, :] = mag * jnp.cos(th)
            pw_im_ref[d, r:r + 1, :] = mag * jnp.sin(th)
        mag = jnp.exp(lr * delta)
        nr = mag * jnp.cos(li * delta) - 1.0
        ni = mag * jnp.sin(li * delta)
        den = lr * lr + li * li
        fr = (nr * lr + ni * li) / den
        fi = (ni * lr - nr * li) / den
        br = brt_ref[d]
        bi = bit_ref[d]
        bb_re_ref[d] = fr * br - fi * bi
        bb_im_ref[d] = fr * bi + fi * br


def _s5_params(lam_re, lam_im, log_step, b_re, b_im, c_re, c_im):
    nd, g, p = lam_re.shape
    gp = g * p
    nb = g // S5_BLOCK_GROUPS
    ls = jnp.repeat(log_step, p, axis=-1).reshape(nd, 1, gp)
    brt = b_re.transpose(0, 3, 1, 2).reshape(nd, S5_GROUP_CH, gp)
    bit = b_im.transpose(0, 3, 1, 2).reshape(nd, S5_GROUP_CH, gp)
    full = lambda *shape: pl.BlockSpec(shape, lambda: (0,) * len(shape))
    pw_re, pw_im, bb_re, bb_im = pl.pallas_call(
        _s5_disc_kernel,
        in_specs=[full(nd, 1, gp), full(nd, 1, gp), full(nd, 1, gp),
                  full(nd, S5_GROUP_CH, gp), full(nd, S5_GROUP_CH, gp)],
        out_specs=[full(nd, len(S5_POWERS), gp), full(nd, len(S5_POWERS), gp),
                   full(nd, S5_GROUP_CH, gp), full(nd, S5_GROUP_CH, gp)],
        out_shape=[jax.ShapeDtypeStruct((nd, len(S5_POWERS), gp), F32)] * 2
        + [jax.ShapeDtypeStruct((nd, S5_GROUP_CH, gp), F32)] * 2,
        name="s5_discretize",
    )(lam_re.reshape(nd, 1, gp), lam_im.reshape(nd, 1, gp), ls, brt, bit)

    eye = jnp.eye(S5_BLOCK_GROUPS, dtype=F32)

    def in_proj(bb):
        bb = bb.reshape(nd, S5_GROUP_CH, nb, S5_BLOCK_GROUPS, p)
        w = jnp.einsum('dhjgp,gk->djghkp', bb, eye)
        return w.reshape(nd, nb, LANES, S5_BLOCK_STATES)

    def out_proj(c):
        c = c.reshape(nd, nb, S5_BLOCK_GROUPS, S5_GROUP_CH, p)
        w = jnp.einsum('djghp,gk->djgpkh', c, eye)
        return w.reshape(nd, nb, S5_BLOCK_STATES, LANES)

    wb = jnp.concatenate([in_proj(bb_re), in_proj(bb_im)], axis=-1).astype(BF16)
    wc = jnp.concatenate([out_proj(c_re), -out_proj(c_im)], axis=-2).astype(BF16)

    rows = jnp.arange(SUBLANES)
    one = (jnp.ones((gp,), F32), jnp.zeros((gp,), F32))
    power = lambda d, m: tuple(pw[d, S5_POWERS.index(m)] for pw in (pw_re, pw_im))
    seg, step = [], []
    for d in range(nd):
        rev = d == 1
        kinds = [jnp.broadcast_to(c, (SUBLANES, gp)) for c in power(d, 1)]
        for sh in (1, 2, 4):
            keep = (rows < SUBLANES - sh) if rev else (rows >= sh)
            kinds += [jnp.where(keep[:, None], c[None, :], 0.0) for c in power(d, sh * S5_SEG)]
        dist = [(SUBLANES - 1 - s) if rev else s for s in range(SUBLANES)]
        per_row = [one if m == 0 else power(d, m * S5_SEG) for m in dist]
        kinds += [jnp.stack([c[part] for c in per_row]) for part in range(2)]
        kinds += [jnp.broadcast_to(c, (SUBLANES, gp)) for c in power(d, SUBLANES * S5_SEG)]
        seg.append(jnp.stack(kinds))
        order = [(S5_SEG - k) if rev else (k + 1) for k in range(S5_SEG)]
        step.append(jnp.stack([jnp.stack([power(d, m)[part] for m in order]) for part in range(2)]))
    coef_seg = jnp.stack(seg).reshape(nd, 12, SUBLANES, nb, S5_BLOCK_STATES).transpose(0, 3, 1, 2, 4)
    coef_step = jnp.stack(step).reshape(nd, 2, S5_SEG, nb, S5_BLOCK_STATES).transpose(0, 3, 1, 2, 4)
    return wb, wc, coef_seg, coef_step


def _s5_scan_kernel(uf_ref, ub_ref, wb_ref, wc_ref, seg_ref, step_ref, yf_ref, yb_ref, bu_ref, carry_ref,
                    *, nb, sq):
    i = pl.program_id(0)
    n = pl.num_programs(0)
    ns = S5_BLOCK_STATES
    cmul = lambda ar, ai, br, bi: (ar * br - ai * bi, ar * bi + ai * br)
    sub = lax.broadcasted_iota(jnp.int32, (SUBLANES, ns), 0)
    for d, (u_ref, y_ref) in enumerate(((uf_ref, yf_ref), (ub_ref, yb_ref))):
        rev = d == 1
        ti = (n - 1 - i) if rev else i
        _, first, last = _tile_seq(ti, S5_TILE, sq)

        @pl.when(last if rev else first)
        def _():
            carry_ref[d] = jnp.zeros(carry_ref.shape[1:], F32)

        ks = range(S5_SEG - 1, -1, -1) if rev else range(S5_SEG)
        edge = 0 if rev else SUBLANES - 1
        for j in range(nb):
            ub = u_ref[:, LANES * j:LANES * (j + 1)].astype(BF16)
            bu_ref[d] = _dot(ub, wb_ref[d, j])
            ar, ai = seg_ref[d, j, 0], seg_ref[d, j, 1]
            sr = jnp.zeros((SUBLANES, ns), F32)
            si = jnp.zeros((SUBLANES, ns), F32)
            for k in ks:
                x = bu_ref[d, SUBLANES * k:SUBLANES * (k + 1), :]
                pr, pi = cmul(ar, ai, sr, si)
                sr, si = pr + x[:, :ns], pi + x[:, ns:]
                bu_ref[d, SUBLANES * k:SUBLANES * (k + 1), :] = jnp.concatenate([sr, si], axis=1)
            for lvl, sh in enumerate((1, 2, 4)):
                s = (SUBLANES - sh) if rev else sh
                pr, pi = cmul(seg_ref[d, j, 2 + 2 * lvl], seg_ref[d, j, 3 + 2 * lvl],
                              pltpu.roll(sr, s, 0), pltpu.roll(si, s, 0))
                sr, si = sr + pr, si + pi
            c0 = carry_ref[d, j]
            cr, ci = c0[:, :ns], c0[:, ns:]
            s1 = (SUBLANES - 1) if rev else 1
            pr, pi = cmul(seg_ref[d, j, 8], seg_ref[d, j, 9], cr, ci)
            in_r = jnp.where(sub == (SUBLANES - 1 - edge), 0.0, pltpu.roll(sr, s1, 0)) + pr
            in_i = jnp.where(sub == (SUBLANES - 1 - edge), 0.0, pltpu.roll(si, s1, 0)) + pi
            pr, pi = cmul(seg_ref[d, j, 10], seg_ref[d, j, 11], cr, ci)
            carry_ref[d, j] = jnp.concatenate(
                [jnp.broadcast_to((sr + pr)[edge:edge + 1, :], (SUBLANES, ns)),
                 jnp.broadcast_to((si + pi)[edge:edge + 1, :], (SUBLANES, ns))], axis=1)
            for k in ks:
                x = bu_ref[d, SUBLANES * k:SUBLANES * (k + 1), :]
                pr, pi = cmul(step_ref[d, j, 0, k:k + 1, :], step_ref[d, j, 1, k:k + 1, :], in_r, in_i)
                bu_ref[d, SUBLANES * k:SUBLANES * (k + 1), :] = jnp.concatenate(
                    [x[:, :ns] + pr, x[:, ns:] + pi], axis=1)
            y_ref[:, LANES * j:LANES * (j + 1)] = _dot(bu_ref[d].astype(BF16), wc_ref[d, j])


def _s5_interleave(x, inverse=False):
    t, w = x.shape
    shape = (t // S5_TILE, S5_SEG, SUBLANES, w) if inverse else (t // S5_TILE, SUBLANES, S5_SEG, w)
    return x.reshape(shape).transpose(0, 2, 1, 3).reshape(t, w)


def _s5_scan_call(u, dm, sq, wb, wc, coef_seg, coef_step):
    t = u.shape[0]
    assert sq.l1 % S5_TILE == 0 and sq.l2 % S5_TILE == 0
    n = t // S5_TILE
    nb = dm.s5w // LANES
    full = lambda a: pl.BlockSpec(a.shape, lambda i: (0,) * a.ndim)
    return pl.pallas_call(
        functools.partial(_s5_scan_kernel, nb=nb, sq=sq),
        grid=(n,),
        in_specs=[pl.BlockSpec((S5_TILE, dm.s5w), lambda i: (i, 0)),
                  pl.BlockSpec((S5_TILE, dm.s5w), lambda i: (n - 1 - i, 0)),
                  full(wb), full(wc), full(coef_seg), full(coef_step)],
        out_specs=[pl.BlockSpec((S5_TILE, dm.s5w), lambda i: (i, 0)),
                   pl.BlockSpec((S5_TILE, dm.s5w), lambda i: (n - 1 - i, 0))],
        out_shape=[jax.ShapeDtypeStruct((t, dm.s5w), F32)] * 2,
        scratch_shapes=[pltpu.VMEM((N_DIR, S5_TILE, 2 * S5_BLOCK_STATES), F32),
                        pltpu.VMEM((N_DIR, nb, SUBLANES, 2 * S5_BLOCK_STATES), F32)],
        compiler_params=_cparams("arbitrary"),
        name="s5_scan",
    )(u, u, wb, wc, coef_seg, coef_step)


def _s5_out_kernel(u_ref, yf_ref, yb_ref, d_ref, w_ref, b_ref, o_ref):
    y = u_ref[...] * d_ref[...] + yf_ref[...] + yb_ref[...]
    y = jax.nn.gelu(y)
    gate = jax.nn.sigmoid(_dot(y.astype(BF16), w_ref[...]) + b_ref[...])
    o_ref[...] = (y * gate).astype(BF16)


def _s5_out_call(u, yf, yb, dm, d_skip, glu_w, glu_b):
    t = u.shape[0]
    w = dm.s5w
    tm = min(512, t)
    row = lambda c: pl.BlockSpec((tm, w), lambda i: (i, c))
    vec = pl.BlockSpec((1, w), lambda i: (0, 0))
    return pl.pallas_call(
        _s5_out_kernel,
        grid=(t // tm,),
        in_specs=[row(0), row(0), row(0), vec, pl.BlockSpec((w, w), lambda i: (0, 0)), vec],
        out_specs=row(0),
        out_shape=jax.ShapeDtypeStruct((t, w), BF16),
        compiler_params=_cparams("parallel"),
        name="s5_out",
    )(u, yf, yb, d_skip.reshape(1, w), glu_w.astype(BF16), glu_b.reshape(1, w))


def _gla_kernel(qf_ref, kf_ref, vf_ref, sf_ref, qb_ref, kb_ref, vb_ref, sb_ref, gu_ref, bias_ref,
                of_ref, ob_ref, st_ref, *, nc, npairs, tb, sq):
    i = pl.program_id(0)
    n = pl.num_programs(0)
    row = lax.broadcasted_iota(jnp.int32, (CHUNK, CHUNK), 0)
    col = lax.broadcasted_iota(jnp.int32, (CHUNK, CHUNK), 1)
    lane = lax.broadcasted_iota(jnp.int32, (CHUNK, LANES), 1)
    first_half = lane < GLA_DK
    row2 = lax.broadcasted_iota(jnp.int32, (2 * CHUNK, LANES), 0)
    col2 = lax.broadcasted_iota(jnp.int32, (2 * CHUNK, LANES), 1)
    scale = GLA_DK ** -0.5
    dirs = ((qf_ref, kf_ref, vf_ref, sf_ref, of_ref), (qb_ref, kb_ref, vb_ref, sb_ref, ob_ref))
    for d, (q_ref, k_ref, v_ref, s_ref, o_ref) in enumerate(dirs):
        rev = d == 1
        ti = (n - 1 - i) if rev else i
        _, first, last = _tile_seq(ti, tb, sq)

        @pl.when(last if rev else first)
        def _():
            st_ref[d] = jnp.zeros(st_ref.shape[1:], F32)

        tri = jnp.where((col >= row) if rev else (col <= row), 1.0, 0.0)
        keep2 = jnp.logical_and((row2 // CHUNK) == (col2 // CHUNK),
                                (col2 % CHUNK > row2 % CHUNK) if rev else (col2 % CHUNK <= row2 % CHUNK))
        for cc in range(nc):
            c = (nc - 1 - cc) if rev else cc
            rows = slice(CHUNK * c, CHUNK * (c + 1))
            z = _dot(s_ref[rows, :], gu_ref[d]) + bias_ref[d]
            log_a = _log_sigmoid(z) * (1.0 / GLA_GATE_NORM)
            b = _dot(tri, log_a, precision=HIGHEST)
            b_last = b[0:1, :] if rev else b[CHUNK - 1:CHUNK, :]
            q = q_ref[rows, :] * scale
            k = k_ref[rows, :]
            q_dec = q * jnp.exp(b)
            k_inv = k * jnp.exp(-b)
            k_dec = k * jnp.exp(b_last - b)
            dec = jnp.exp(b_last)
            for p in range(npairs):
                sl = slice(LANES * p, LANES * (p + 1))
                qp, kip, kdp = q_dec[:, sl], k_inv[:, sl], k_dec[:, sl]
                st = st_ref[d, p]
                q2 = jnp.concatenate([jnp.where(first_half, qp, 0.0), jnp.where(first_half, 0.0, qp)], axis=0)
                kd2 = jnp.concatenate([jnp.where(first_half, kdp, 0.0), jnp.where(first_half, 0.0, kdp)], axis=0)
                attn = jnp.where(keep2, _dot_nt(q2, jnp.concatenate([kip, kip], axis=0)), 0.0)
                vt = v_ref[rows, GLA_DV * 2 * p:GLA_DV * 2 * (p + 1)]
                vt = jnp.concatenate([vt[:, :GLA_DV], vt[:, GLA_DV:]], axis=0).T
                o2 = _dot_nt(jnp.concatenate([q2, attn], axis=1), jnp.concatenate([st, vt], axis=1))
                o_ref[rows, GLA_DV * 2 * p:GLA_DV * (2 * p + 1)] = o2[:CHUNK]
                o_ref[rows, GLA_DV * (2 * p + 1):GLA_DV * (2 * p + 2)] = o2[CHUNK:]
                st_ref[d, p] = st * dec[:, sl] + _dot(vt, kd2)


def _gla_call(proj, dm, sq, gate_up, gate_bias):
    t = proj.shape[0]
    nc = 4
    tb = min(nc * CHUNK, sq.l1, sq.l2)
    nc = tb // CHUNK
    n = t // tb
    npairs = dm.gla_h // 2
    qk_w = dm.gla_h * GLA_DK
    gu = jnp.zeros((N_DIR, LANES, qk_w), F32)
    for d in range(N_DIR):
        gu = gu.at[d, d * GLA_LOW_RANK:(d + 1) * GLA_LOW_RANK, :].set(gate_up[d])
    bias = gate_bias.reshape(N_DIR, 1, qk_w)

    def specs(tile):
        return [pl.BlockSpec((tb, qk_w), lambda i: (tile(i), dm.o_gq // qk_w)),
                pl.BlockSpec((tb, qk_w), lambda i: (tile(i), dm.o_gk // qk_w)),
                pl.BlockSpec((tb, dm.gla_w), lambda i: (tile(i), dm.o_gv // dm.gla_w)),
                pl.BlockSpec((tb, LANES), lambda i: (tile(i), dm.o_small // LANES))]

    fwd = lambda i: i
    bwd = lambda i: n - 1 - i
    full = lambda a: pl.BlockSpec(a.shape, lambda i: (0,) * a.ndim)
    return pl.pallas_call(
        functools.partial(_gla_kernel, nc=nc, npairs=npairs, tb=tb, sq=sq),
        grid=(n,),
        in_specs=specs(fwd) + specs(bwd) + [full(gu), full(bias)],
        out_specs=[pl.BlockSpec((tb, dm.gla_w), lambda i: (i, 0)),
                   pl.BlockSpec((tb, dm.gla_w), lambda i: (n - 1 - i, 0))],
        out_shape=[jax.ShapeDtypeStruct((t, dm.gla_w), F32)] * 2,
        scratch_shapes=[pltpu.VMEM((N_DIR, npairs, GLA_DV, LANES), F32)],
        compiler_params=_cparams("arbitrary"),
        name="gla_chunks",
    )(*([proj] * 8), gu, bias)


def _gla_out_kernel(of_ref, ob_ref, g_ref, gain_ref, o_ref, *, heads):
    for h in range(heads):
        sl = slice(GLA_DV * h, GLA_DV * (h + 1))
        o = of_ref[:, sl] + ob_ref[:, sl]
        r = lax.rsqrt(jnp.mean(o * o, axis=-1, keepdims=True) + NORM_EPS)
        o_ref[:, sl] = (o * r * gain_ref[...] * _silu(g_ref[:, sl])).astype(BF16)


def _gla_out_call(proj, of, ob, dm, gain):
    t = proj.shape[0]
    w = dm.gla_w
    tm = min(512, t)
    row = lambda c: pl.BlockSpec((tm, w), lambda i: (i, c))
    return pl.pallas_call(
        functools.partial(_gla_out_kernel, heads=dm.gla_h),
        grid=(t // tm,),
        in_specs=[row(0), row(0), row(dm.o_gg // w), pl.BlockSpec((1, GLA_DV), lambda i: (0, 0))],
        out_specs=row(0),
        out_shape=jax.ShapeDtypeStruct((t, w), BF16),
        compiler_params=_cparams("parallel"),
        name="gla_out",
    )(of, ob, proj, gain.reshape(1, GLA_DV))


GDN_BETA_LANE = N_DIR * GLA_LOW_RANK
GDN_HEADS_PER_STEP = 3


def _paired_unit_triangular_solve(a, xs, upper, lo):
    nblk = CHUNK // SUBLANES
    a_blk = [a[SUBLANES * i:SUBLANES * (i + 1), :] for i in range(nblk)]
    x_blk = [[x[SUBLANES * i:SUBLANES * (i + 1), :] for i in range(nblk)] for x in xs]
    order = range(CHUNK - 1, 0, -1) if upper else range(CHUNK - 1)
    for j in order:
        bj, rj = divmod(j, SUBLANES)
        idx = jnp.where(lo, j, CHUNK + j)
        rows = [xb[bj][rj:rj + 1, :] for xb in x_blk]
        for i in (range(bj + 1) if upper else range(bj, nblk)):
            col = jnp.take_along_axis(a_blk[i], idx, axis=1)
            for xb, x_row in zip(x_blk, rows):
                xb[i] = xb[i] - col * x_row
    return [jnp.concatenate(xb, axis=0) for xb in x_blk]


def _gdn_wy_kernel(x_ref, prev_ref, next_ref, s_ref, cw_ref, alog_ref, dtb_ref,
                   u_ref, w_ref, qd_ref, kd_ref, qk_ref, gc_ref,
                   ext_ref, q_ref, k_ref, v_ref, beta_ref, gcx_ref, gct_ref, *, tm, heads, sq):
    i = pl.program_id(0)
    _, first, last = _tile_seq(i, tm, sq)
    halo = SUBLANES
    pad = GDN_CONV // 2
    w = heads * GDN_DK
    ext_ref[0:halo, :] = jnp.where(first, 0.0, prev_ref[...])
    ext_ref[halo:halo + tm, :] = x_ref[...]
    ext_ref[halo + tm:, :] = jnp.where(last, 0.0, next_ref[...])
    acc = cw_ref[0:1, :] * ext_ref[pl.ds(halo - pad, tm), :]
    for j in range(1, GDN_CONV):
        acc = acc + cw_ref[j:j + 1, :] * ext_ref[pl.ds(halo - pad + j, tm), :]
    y = _silu(acc)
    for h in range(heads):
        qh = y[:, GDN_DK * h:GDN_DK * (h + 1)]
        kh = y[:, w + GDN_DK * h:w + GDN_DK * (h + 1)]
        q_ref[h] = qh * lax.rsqrt(jnp.sum(qh * qh, axis=-1, keepdims=True) + NORM_EPS) * (GDN_DK ** -0.5)
        k_ref[h] = kh * lax.rsqrt(jnp.sum(kh * kh, axis=-1, keepdims=True) + NORM_EPS)
        v_ref[h] = y[:, 2 * w + GDN_DV * h:2 * w + GDN_DV * (h + 1)]
    sm = s_ref[...]
    beta = jax.nn.sigmoid(sm)
    g = -jnp.exp(alog_ref[...]) * _softplus(sm + dtb_ref[...])
    row = lax.broadcasted_iota(jnp.int32, (tm, tm), 0)
    col = lax.broadcasted_iota(jnp.int32, (tm, tm), 1)
    same = (row // CHUNK) == (col // CHUNK)
    a_lane = GDN_BETA_LANE + N_DIR * heads
    for d in range(N_DIR):
        tri = jnp.where(jnp.logical_and(same, (col >= row) if d == 1 else (col <= row)), 1.0, 0.0)
        gc = _dot(tri, g, precision=HIGHEST)
        gc_ref[d] = gc
        for h in range(heads):
            lb = GDN_BETA_LANE + d * heads + h
            la = a_lane + d * heads + h
            beta_ref[d, h] = jnp.broadcast_to(beta[:, lb:lb + 1], (tm, LANES))
            gcx_ref[d, h] = jnp.broadcast_to(gc[:, la:la + 1], (tm, LANES))
        for c in range(tm // CHUNK):
            gcm = gc[CHUNK * c:CHUNK * (c + 1), :]
            zero = jnp.zeros_like(gcm)
            gct_ref[d, c] = jnp.concatenate([gcm, zero] if c == 0 else [zero, gcm], axis=0).T

    row = lax.broadcasted_iota(jnp.int32, (CHUNK, LANES), 0)
    lane = lax.broadcasted_iota(jnp.int32, (CHUNK, LANES), 1)
    lo = lane < CHUNK
    t_col = jnp.where(lo, lane, lane - CHUNK)
    lo_idx = lax.broadcasted_iota(jnp.int32, (SUBLANES, LANES), 1) < CHUNK

    def halves(x):
        return x[:CHUNK], x[CHUNK:]

    def pack(x):
        top, bot = halves(x)
        return [jnp.where(lo, top, pltpu.roll(bot, CHUNK, 1)),
                jnp.where(lo, pltpu.roll(top, CHUNK, 1), bot)]

    def unpack(p1, p2):
        return jnp.concatenate([jnp.where(lo, p1, pltpu.roll(p2, CHUNK, 1)),
                                jnp.where(lo, pltpu.roll(p1, CHUNK, 1), p2)], axis=0)

    def diag_blocks(x):
        top, bot = halves(x)
        return jnp.where(lo, top, bot)

    def body(hp, carry):
        loaded = []
        for j in range(GDN_HEADS_PER_STEP):
            h = hp * GDN_HEADS_PER_STEP + j
            per_dir = [(beta_ref[d, h], gcx_ref[d, h],
                        gct_ref[d, 0, pl.ds(a_lane + d * heads + h, 1), :]
                        + gct_ref[d, 1, pl.ds(a_lane + d * heads + h, 1), :]) for d in range(N_DIR)]
            loaded.append((h, q_ref[h], k_ref[h], v_ref[h], per_dir))
        stores = []
        for h, q, k, v, per_dir in loaded:
            gram = diag_blocks(_dot_nt(k, k))
            qkt = diag_blocks(_dot_nt(q, k))
            for d, (bt, gc, gc_row) in enumerate(per_dir):
                rev = d == 1
                incl = (t_col >= row) if rev else (t_col <= row)
                strict = (t_col > row) if rev else (t_col < row)
                decay = jnp.exp(jnp.where(incl, diag_blocks(gc) - gc_row, -jnp.inf))
                a = jnp.where(strict, diag_blocks(bt) * gram * decay, 0.0)
                kb = k * bt
                xs = _paired_unit_triangular_solve(a, pack(v * bt) + pack(kb * jnp.exp(gc)), rev, lo_idx)
                e = 0 if rev else CHUNK - 1
                gc_last = jnp.concatenate(
                    [jnp.broadcast_to(gc[e:e + 1, :], (CHUNK, LANES)),
                     jnp.broadcast_to(gc[CHUNK + e:CHUNK + e + 1, :], (CHUNK, LANES))], axis=0)
                qk = jnp.where(strict if rev else incl, qkt * decay, 0.0)
                stores.append((d, h, unpack(xs[0], xs[1]), unpack(xs[2], xs[3]).astype(BF16),
                               (q * jnp.exp(gc)).astype(BF16), (k * jnp.exp(gc_last - gc)).T.astype(BF16),
                               qk.astype(BF16)))
        for d, h, u, w, qd, kd, qk in stores:
            u_ref[d, h] = u
            w_ref[d, h] = w
            qd_ref[d, h] = qd
            kd_ref[d, h] = kd
            qk_ref[d, h] = qk
        return carry

    lax.fori_loop(0, heads // GDN_HEADS_PER_STEP, body, 0)


def _gdn_wy_call(proj, dm, sq, conv_w, a_log, dt_bias):
    t = proj.shape[0]
    heads = dm.gdn_h
    tm = 2 * CHUNK
    assert sq.l1 % tm == 0 and sq.l2 % tm == 0 and heads % GDN_HEADS_PER_STEP == 0
    n = t // tm
    wq = 3 * dm.gdn_w
    hb = tm // SUBLANES
    a_lane = GDN_BETA_LANE + N_DIR * heads
    alog = jnp.zeros((1, LANES), F32).at[0, a_lane:a_lane + N_DIR * heads].set(a_log.reshape(-1))
    dtb = jnp.zeros((1, LANES), F32).at[0, a_lane:a_lane + N_DIR * heads].set(dt_bias.reshape(-1))
    cw = conv_w.reshape(GDN_CONV, wq)
    cb = dm.o_qkv // wq
    nhb = t // SUBLANES
    dhm = lambda dt: jax.ShapeDtypeStruct((N_DIR, heads, t, LANES), dt)
    dhm_spec = pl.BlockSpec((N_DIR, heads, tm, LANES), lambda i: (0, 0, i, 0))
    return pl.pallas_call(
        functools.partial(_gdn_wy_kernel, tm=tm, heads=heads, sq=sq),
        grid=(n,),
        in_specs=[pl.BlockSpec((tm, wq), lambda i: (i, cb)),
                  pl.BlockSpec((SUBLANES, wq), lambda i: (jnp.maximum(i * hb - 1, 0), cb)),
                  pl.BlockSpec((SUBLANES, wq), lambda i: (jnp.minimum((i + 1) * hb, nhb - 1), cb)),
                  pl.BlockSpec((tm, LANES), lambda i: (i, dm.o_small // LANES)),
                  pl.BlockSpec((GDN_CONV, wq), lambda i: (0, 0)),
                  pl.BlockSpec((1, LANES), lambda i: (0, 0)),
                  pl.BlockSpec((1, LANES), lambda i: (0, 0))],
        out_specs=[dhm_spec] * 4
        + [pl.BlockSpec((N_DIR, heads, CHUNK, LANES), lambda i: (0, 0, i, 0)),
           pl.BlockSpec((N_DIR, tm, LANES), lambda i: (0, i, 0))],
        out_shape=[dhm(F32)] + [dhm(BF16)] * 3
        + [jax.ShapeDtypeStruct((N_DIR, heads, t // 2, LANES), BF16),
           jax.ShapeDtypeStruct((N_DIR, t, LANES), F32)],
        scratch_shapes=[pltpu.VMEM((tm + 2 * SUBLANES, wq), F32)]
        + [pltpu.VMEM((heads, tm, LANES), F32)] * 3
        + [pltpu.VMEM((N_DIR, heads, tm, LANES), F32)] * 2
        + [pltpu.VMEM((N_DIR, tm // CHUNK, LANES, LANES), F32)],
        compiler_params=_cparams("parallel"),
        name="gdn_wy",
    )(proj, proj, proj, proj, cw, alog, dtb)


def _gdn_scan_kernel(uf_ref, wf_ref, qdf_ref, kdf_ref, qkf_ref, gcf_ref,
                     ub_ref, wb_ref, qdb_ref, kdb_ref, qkb_ref, gcb_ref,
                     of_ref, ob_ref, s_ref, *, nc, heads, tb, sq):
    i = pl.program_id(0)
    n = pl.num_programs(0)
    a_lane = GDN_BETA_LANE + N_DIR * heads
    dirs = ((uf_ref, wf_ref, qdf_ref, kdf_ref, qkf_ref, gcf_ref, of_ref),
            (ub_ref, wb_ref, qdb_ref, kdb_ref, qkb_ref, gcb_ref, ob_ref))
    for d in range(N_DIR):
        rev = d == 1
        ti = (n - 1 - i) if rev else i
        _, first, last = _tile_seq(ti, tb, sq)

        @pl.when(last if rev else first)
        def _():
            s_ref[d] = jnp.zeros(s_ref.shape[1:], F32)

    lo = lax.broadcasted_iota(jnp.int32, (CHUNK + GDN_DK, LANES), 1) < CHUNK
    zeros = jnp.zeros((CHUNK, GDN_DV), BF16)
    for cc in range(nc):
        for d, (u_ref, w_ref, qd_ref, kd_ref, qk_ref, gc_ref, o_ref) in enumerate(dirs):
            rev = d == 1
            c = (nc - 1 - cc) if rev else cc
            rows = slice(CHUNK * c, CHUNK * (c + 1))
            e = CHUNK * c + (0 if rev else CHUNK - 1)
            for h in range(heads):
                state = s_ref[d, h]
                wq = _dot(jnp.concatenate([w_ref[0, h, rows, :], qd_ref[0, h, rows, :]], axis=0),
                          state.astype(BF16))
                v_new = (u_ref[0, h, rows, :] - wq[:CHUNK]).astype(BF16)
                both = jnp.concatenate([qk_ref[0, h, CHUNK * (c // 2):CHUNK * (c // 2 + 1), :],
                                        kd_ref[0, h, 2 * CHUNK * (c // 2):2 * CHUNK * (c // 2 + 1), :]], axis=0)
                if c % 2 == 0:
                    res = _dot(jnp.where(lo, both, 0), jnp.concatenate([v_new, zeros], axis=0))
                else:
                    res = _dot(jnp.where(lo, 0, both), jnp.concatenate([zeros, v_new], axis=0))
                o_ref[h, rows, :] = wq[CHUNK:] + res[:CHUNK]
                la = a_lane + d * heads + h
                dec = jnp.exp(gc_ref[0, e:e + 1, la:la + 1])
                s_ref[d, h] = state * dec + res[CHUNK:]


def _gdn_scan_call(u, w, qd, kd, qk, gc, dm, sq):
    _, heads, t, _ = u.shape
    nc = 2
    tb = min(nc * CHUNK, sq.l1, sq.l2)
    nc = tb // CHUNK
    n = t // tb
    fwd = lambda i: i
    bwd = lambda i: n - 1 - i
    hm = lambda tile: pl.BlockSpec((heads, tb, LANES), lambda i: (0, tile(i), 0))
    dhm = lambda d, tile: pl.BlockSpec((1, heads, tb, LANES), lambda i: (d, 0, tile(i), 0))
    cm = lambda d, tile: pl.BlockSpec((1, tb, LANES), lambda i: (d, tile(i), 0))
    qkm = lambda d, tile: pl.BlockSpec((1, heads, tb // 2, LANES), lambda i: (d, 0, tile(i), 0))
    per_dir = lambda d, tile: [dhm(d, tile)] * 4 + [qkm(d, tile), cm(d, tile)]
    return pl.pallas_call(
        functools.partial(_gdn_scan_kernel, nc=nc, heads=heads, tb=tb, sq=sq),
        grid=(n,),
        in_specs=per_dir(0, fwd) + per_dir(1, bwd),
        out_specs=[hm(fwd), hm(bwd)],
        out_shape=[jax.ShapeDtypeStruct((heads, t, LANES), F32)] * 2,
        scratch_shapes=[pltpu.VMEM((N_DIR, heads, GDN_DK, GDN_DV), F32)],
        compiler_params=_cparams("arbitrary"),
        name="gdn_scan",
    )(u, w, qd, kd, qk, gc, u, w, qd, kd, qk, gc)


def _gdn_out_kernel(of_ref, ob_ref, z_ref, gain_ref, o_ref, *, heads):
    for h in range(heads):
        sl = slice(GDN_DV * h, GDN_DV * (h + 1))
        o = of_ref[h] + ob_ref[h]
        r = lax.rsqrt(jnp.mean(o * o, axis=-1, keepdims=True) + NORM_EPS)
        o_ref[:, sl] = (o * r * gain_ref[...] * _silu(z_ref[:, sl])).astype(BF16)


def _gdn_out_call(proj, of, ob, dm, gain):
    t = proj.shape[0]
    w = dm.gdn_w
    heads = dm.gdn_h
    tm = min(512, t)
    hm = pl.BlockSpec((heads, tm, LANES), lambda i: (0, i, 0))
    return pl.pallas_call(
        functools.partial(_gdn_out_kernel, heads=heads),
        grid=(t // tm,),
        in_specs=[hm, hm, pl.BlockSpec((tm, w), lambda i: (i, dm.o_z // w)),
                  pl.BlockSpec((1, GDN_DV), lambda i: (0, 0))],
        out_specs=pl.BlockSpec((tm, w), lambda i: (i, 0)),
        out_shape=jax.ShapeDtypeStruct((t, w), BF16),
        compiler_params=_cparams("parallel"),
        name="gdn_out",
    )(of, ob, proj, gain.reshape(1, GDN_DV))


def _ffn_act_kernel(g_ref, gp_ref, gn_ref, v_ref, vp_ref, vn_ref, cwg_ref, cwv_ref, o_ref, *, tm, sq):
    i = pl.program_id(0)
    _, first, last = _tile_seq(i, tm, sq)
    sub = min(LANES, tm)
    row = lax.broadcasted_iota(jnp.int32, (2 * sub, sub), 0)
    col = lax.broadcasted_iota(jnp.int32, (2 * sub, sub), 1)
    shifts = jnp.where(col == jnp.where(row < sub, row - 1, row - sub + 1), 1.0, 0.0).astype(BF16)
    edge = lax.broadcasted_iota(jnp.int32, (SUBLANES, 1), 0)

    def conv(x_ref, prev_ref, next_ref, cw_ref):
        xb = x_ref[...]
        xf = xb.astype(F32)
        prevs, nexts = [], []
        for s in range(tm // sub):
            r0 = sub * s
            xs = _dot(shifts, xb[r0:r0 + sub])
            before = (jnp.where(first, 0.0, prev_ref[BF16_ROWS - 1:BF16_ROWS, :].astype(F32)) if s == 0
                      else xf[r0 - 1:r0])
            after = (jnp.where(last, 0.0, next_ref[0:1, :].astype(F32)) if r0 + sub == tm
                     else xf[r0 + sub:r0 + sub + 1])
            prevs += [xs[0:SUBLANES] + jnp.where(edge == 0, before, 0.0), xs[SUBLANES:sub]]
            nexts += [xs[sub:2 * sub - SUBLANES],
                      xs[2 * sub - SUBLANES:] + jnp.where(edge == SUBLANES - 1, after, 0.0)]
        x_prev = jnp.concatenate(prevs, axis=0)
        x_next = jnp.concatenate(nexts, axis=0)
        return cw_ref[0:1, :] * x_prev + cw_ref[1:2, :] * xf + cw_ref[2:3, :] * x_next

    gate = conv(g_ref, gp_ref, gn_ref, cwg_ref)
    val = conv(v_ref, vp_ref, vn_ref, cwv_ref)
    o_ref[...] = (_silu(gate) * val).astype(BF16)


def _ffn_act_call(h1, dm, sq, cw_gate, cw_val):
    t = h1.shape[0]
    assert FFN_CONV == 3
    f = dm.dffp
    tm = min(512, sq.l1, sq.l2)
    tc = min(MM_TILE, f)
    hb = tm // BF16_ROWS
    nhb = t // BF16_ROWS
    nv = f // tc
    main = lambda off: pl.BlockSpec((tm, tc), lambda i, j: (i, j + off))
    prev = lambda off: pl.BlockSpec((BF16_ROWS, tc), lambda i, j: (jnp.maximum(i * hb - 1, 0), j + off))
    nxt = lambda off: pl.BlockSpec((BF16_ROWS, tc), lambda i, j: (jnp.minimum((i + 1) * hb, nhb - 1), j + off))
    cws = pl.BlockSpec((FFN_CONV, tc), lambda i, j: (0, j))
    return pl.pallas_call(
        functools.partial(_ffn_act_kernel, tm=tm, sq=sq),
        grid=(t // tm, f // tc),
        in_specs=[main(0), prev(0), nxt(0), main(nv), prev(nv), nxt(nv), cws, cws],
        out_specs=pl.BlockSpec((tm, tc), lambda i, j: (i, j)),
        out_shape=jax.ShapeDtypeStruct((t, f), BF16),
        compiler_params=_cparams("parallel", "parallel"),
        name="ffn_conv_swiglu",
    )(h1, h1, h1, h1, h1, h1, cw_gate, cw_val)


def _pad_cols(w, n):
    return jnp.pad(w, ((0, 0), (0, n - w.shape[1])))


def _layout_w_in(w_in, dm):
    qk = dm.gla_h * GLA_DK
    widths = (dm.s5w, qk, qk, dm.gla_w, dm.gla_w, N_DIR * GLA_LOW_RANK, 3 * dm.gdn_w, dm.gdn_w,
              N_DIR * dm.gdn_h, N_DIR * dm.gdn_h)
    splits = np.cumsum(widths)[:-1]
    s5_u, gq, gk, gv, gg, glr, qkv, z, beta, a = jnp.split(w_in, splits, axis=1)
    small = _pad_cols(jnp.concatenate([glr, beta, a], axis=1), LANES)
    rows = w_in.shape[0]
    parts, cur = [], 0
    for off, blk in sorted(((dm.o_qkv, qkv), (dm.o_z, z), (dm.o_gv, gv), (dm.o_gg, gg), (dm.o_gq, gq),
                            (dm.o_gk, gk), (dm.o_small, small), (dm.o_s5, s5_u)), key=lambda p: p[0]):
        if off > cur:
            parts.append(jnp.zeros((rows, off - cur), BF16))
        parts.append(blk.astype(BF16))
        cur = off + blk.shape[1]
    if cur < dm.n_in_pad:
        parts.append(jnp.zeros((rows, dm.n_in_pad - cur), BF16))
    return jnp.concatenate(parts, axis=1)


def _layer(x, sq, dm, mod, norm_gains, w_in, s5, gla, gdn, w_out, ffn_up, ffn_conv, ffn_down,
           pending):
    d = dm.d
    mod3 = mod.reshape(mod.shape[0], 1, 6 * d)
    gains3 = norm_gains.reshape(4, 1, d)
    if pending is None:
        (h,) = _resnorm_call(x, sq, norm=(mod3, 1, 0, gains3, 0))
    else:
        x, h = _resnorm_call(x, sq, res=pending, norm=(mod3, 1, 0, gains3, 0))
    proj = _mm_call(h, _layout_w_in(w_in, dm), F32, "in_proj")

    u_s5 = _s5_interleave(proj[:, dm.o_s5:dm.o_s5 + dm.s5w])
    yf, yb = _s5_scan_call(u_s5, dm, sq, *_s5_params(*s5[:7]))
    y_s5 = _s5_interleave(_s5_out_call(u_s5, yf, yb, dm, s5[7], s5[8], s5[9]), inverse=True)

    of, ob = _gla_call(proj, dm, sq, gla[0], gla[1])
    y_gla = _gla_out_call(proj, of, ob, dm, gla[2])

    of, ob = _gdn_scan_call(*_gdn_wy_call(proj, dm, sq, gdn[0], gdn[1], gdn[2]), dm, sq)
    y_gdn = _gdn_out_call(proj, of, ob, dm, gdn[3])

    mixed = _mm_call(jnp.concatenate([y_s5, y_gla, y_gdn], axis=1), w_out.astype(BF16), F32, "out_proj")
    x, h = _resnorm_call(x, sq, res=(mixed, mod3, 2, gains3, 1), norm=(mod3, 4, 3, gains3, 2))
    f, fp = dm.dff, dm.dffp
    up_b = ffn_up.astype(BF16)
    up = jnp.concatenate([_pad_cols(up_b[:, :f], fp), _pad_cols(up_b[:, f:], fp)], axis=1)
    cw = ffn_conv.reshape(FFN_CONV, 2 * f)
    h1 = _mm_call(h, up, BF16, "ffn_up")
    act = _ffn_act_call(h1, dm, sq, _pad_cols(cw[:, :f], fp), _pad_cols(cw[:, f:], fp))
    down = jnp.pad(ffn_down.astype(BF16), ((0, fp - f), (0, 0)))
    f_out = _mm_call(act, down, F32, "ffn_down")
    return x, (f_out, mod3, 5, gains3, 3)


def kernel(x_prompt, x_sample, c_prompt, c_sample, ada_w, ada_b, norm_gains, w_in, s5_lambda_re, s5_lambda_im, s5_log_step, s5_b_re, s5_b_im, s5_c_re, s5_c_im, s5_d, s5_glu_w, s5_glu_b, gla_gate_up, gla_gate_bias, gla_norm, gdn_conv, gdn_a_log, gdn_dt_bias, gdn_norm, w_out, ffn_up, ffn_conv, ffn_down):
    b1, l1, d = x_prompt.shape
    b2, l2, _ = x_sample.shape
    sq = _Seq(b1, l1, b2, l2)
    dm = _make_dims(d)
    depth = ada_w.shape[0]
    x = jnp.concatenate([x_prompt.reshape(b1 * l1, d), x_sample.reshape(b2 * l2, d)], axis=0)
    c = jnp.concatenate([c_prompt, c_sample], axis=0)
    c = jnp.pad(c, ((0, _round_up(sq.nseq, 2 * SUBLANES) - sq.nseq), (0, 0)))
    mod = _mod_call(c, ada_w, ada_b)
    pending = None
    for l in range(depth):
        s5 = (s5_lambda_re[l], s5_lambda_im[l], s5_log_step[l], s5_b_re[l], s5_b_im[l], s5_c_re[l],
              s5_c_im[l], s5_d[l], s5_glu_w[l], s5_glu_b[l])
        gla = (gla_gate_up[l], gla_gate_bias[l], gla_norm[l])
        gdn = (gdn_conv[l], gdn_a_log[l], gdn_dt_bias[l], gdn_norm[l])
        x, pending = _layer(x, sq, dm, mod[l], norm_gains[l], w_in[l], s5, gla, gdn, w_out[l],
                            ffn_up[l], ffn_conv[l], ffn_down[l], pending)
    (x,) = _resnorm_call(x, sq, res=pending)
    return (x[:b1 * l1].reshape(b1, l1, d), x[b1 * l1:].reshape(b2, l2, d))
```
